```python
import jax, jax.numpy as jnp
from jax import lax
import numpy as np

D_MODEL = 1024
BATCH = 8
SEQ = 2048
DEPTH = 2
DEC_BATCH = 16
DEC_SEQ = 64
PAST_LEN = 4096

CHUNK = 64
N_META = 16
N_MIXERS = 2
HEAD_DIM = 64
N_HEADS = D_MODEL // HEAD_DIM
D_FF = 2752
LORA_DECAY = 64
LORA_AAA = 64
LORA_GATE = 128
Q_BLOCK = 128
RMS_EPS = 1e-6
GN_EPS = 64e-5

kernel_name = "rwkv7_stickbreaking_macaron_stream"


def rms_norm(x, g):
    xf = x.astype(jnp.float32)
    y = xf * lax.rsqrt(jnp.mean(xf * xf, axis=-1, keepdims=True) + RMS_EPS)
    return (y * g.astype(jnp.float32)).astype(x.dtype)


def swiglu(x, w_in, w_out):
    gate, up = jnp.split(x @ w_in, 2, axis=-1)
    return (jax.nn.silu(gate) * up) @ w_out


def rwkv7_time_mix(h, shift_prev, S0, mu, w_rkv, w0, w1, w2, a0, a1, a2, g1, g2,
                   k_k, k_a, r_k, gn_w, gn_b, w_o):
    B, T, D = h.shape
    H, N = N_HEADS, HEAD_DIM
    f32 = jnp.float32
    h_prev = jnp.concatenate([shift_prev[:, None, :].astype(h.dtype), h[:, :-1]], axis=1)
    dx = h_prev - h
    xr, xw, xk, xv, xa, xg = (h + dx * mu[i] for i in range(6))
    r = xr @ w_rkv[0]
    k = xk @ w_rkv[1]
    v = xv @ w_rkv[2]
    log_rate = -jax.nn.softplus(-(w0 + jnp.tanh(xw @ w1) @ w2)) - 0.5
    decay = jnp.exp(-jnp.exp(log_rate.astype(f32)))
    a = jax.nn.sigmoid((a0 + (xa @ a1) @ a2).astype(f32))
    g = jax.nn.sigmoid(xg @ g1) @ g2
    heads = lambda t: t.astype(f32).reshape(B, T, H, N)
    r, k, v, decay, a = map(heads, (r, k, v, decay, a))
    kk = k * k_k.astype(f32).reshape(H, N)
    kk = kk / jnp.maximum(jnp.sqrt(jnp.sum(kk * kk, axis=-1, keepdims=True)), 1e-12)
    k = k * (1.0 + (a - 1.0) * k_a.astype(f32).reshape(H, N))

    def step(S, inp):
        r_t, w_t, k_t, v_t, kk_t, a_t = inp
        sa = jnp.einsum('bhvk,bhk->bhv', S, -kk_t)
        S = (S * w_t[:, :, None, :]
             + sa[..., None] * (kk_t * a_t)[:, :, None, :]
             + v_t[..., None] * k_t[:, :, None, :])
        return S, jnp.einsum('bhvk,bhk->bhv', S, r_t)

    seq = tuple(jnp.swapaxes(t, 0, 1) for t in (r, decay, k, v, kk, a))
    S_fin, o = lax.scan(step, S0.astype(f32), seq)
    o = jnp.swapaxes(o, 0, 1)
    mean = jnp.mean(o, axis=-1, keepdims=True)
    var = jnp.mean(jnp.square(o - mean), axis=-1, keepdims=True)
    o = ((o - mean) * lax.rsqrt(var + GN_EPS) * gn_w.astype(f32).reshape(H, N)
         + gn_b.astype(f32).reshape(H, N))
    o = o + jnp.sum(r * k * r_k.astype(f32), axis=-1, keepdims=True) * v
    out = (o.reshape(B, T, D).astype(h.dtype) * g) @ w_o
    return out, S_fin.astype(S0.dtype), h[:, -1]


def sb_block(q, q_pos, k, v, k_pos):
    z = jnp.einsum('bhqd,bhkd->bhqk', q, k, preferred_element_type=jnp.float32) * (HEAD_DIM ** -0.5)
    vis = k_pos[None, :] < q_pos[:, None]
    log_beta = jax.nn.log_sigmoid(z)
    log_1m = jnp.where(vis, jax.nn.log_sigmoid(-z), 0.0)
    later = lax.cumsum(log_1m, axis=3, reverse=True) - log_1m
    A = jnp.where(vis, jnp.exp(log_beta + later), 0.0)
    return jnp.einsum('bhqk,bhkd->bhqd', A.astype(v.dtype), v)


def stick_breaking_mix(h, k_past, v_past, w_qkv, q_norm_g, k_norm_g, w_o):
    B, T, D = h.shape
    qkv = (h @ w_qkv).reshape(B, T, 3, N_HEADS, HEAD_DIM)
    q = rms_norm(qkv[:, :, 0], q_norm_g).transpose(0, 2, 1, 3)
    k = rms_norm(qkv[:, :, 1], k_norm_g).transpose(0, 2, 1, 3)
    v = qkv[:, :, 2].transpose(0, 2, 1, 3)
    if k_past is None:
        n_blk = -(-T // Q_BLOCK)
        Lp = n_blk * Q_BLOCK
        pad = ((0, 0), (0, 0), (0, Lp - T), (0, 0))
        qp, kp, vp = jnp.pad(q, pad), jnp.pad(k, pad), jnp.pad(v, pad)
        pos = jnp.arange(Lp)
        q_blocks = qp.reshape(B, N_HEADS, n_blk, Q_BLOCK, HEAD_DIM).transpose(2, 0, 1, 3, 4)
        pos_blocks = pos.reshape(n_blk, Q_BLOCK)
        o = lax.map(lambda args: sb_block(args[0], args[1], kp, vp, pos), (q_blocks, pos_blocks))
        o = o.transpose(1, 2, 0, 3, 4).reshape(B, N_HEADS, Lp, HEAD_DIM)[:, :, :T]
    else:
        P = k_past.shape[2]
        k_all = jnp.concatenate([k_past.astype(k.dtype), k], axis=2)
        v_all = jnp.concatenate([v_past.astype(v.dtype), v], axis=2)
        o = sb_block(q, P + jnp.arange(T), k_all, v_all, jnp.arange(P + T))
    out = o.transpose(0, 2, 1, 3).reshape(B, T, D) @ w_o
    return out, k, v


def layer_stack(x, rwkv_S0, rwkv_shift0, k_past, v_past, ffn_norm_g, ffn_w_in, ffn_w_out,
                mix_norm_g, rwkv_params, sb_params):
    mu, w_rkv, w0, w1, w2, a0, a1, a2, g1, g2, k_k, k_a, r_k, gn_w, gn_b, rw_o = rwkv_params
    w_qkv, q_norm_g, k_norm_g, sb_o = sb_params
    rwkv_S, rwkv_shift, k_new, v_new = rwkv_S0, rwkv_shift0, None, None
    for i in range(DEPTH):
        x = x + 0.5 * swiglu(rms_norm(x, ffn_norm_g[i, 0]), ffn_w_in[i, 0], ffn_w_out[i, 0])
        h = rms_norm(x, mix_norm_g[i])
        if i % N_MIXERS == 0:
            y, rwkv_S, rwkv_shift = rwkv7_time_mix(h, rwkv_shift0, rwkv_S0, mu, w_rkv, w0, w1, w2,
                                                   a0, a1, a2, g1, g2, k_k, k_a, r_k, gn_w, gn_b, rw_o)
        else:
            y, k_new, v_new = stick_breaking_mix(h, k_past, v_past, w_qkv, q_norm_g, k_norm_g, sb_o)
        x = x + y
        x = x + 0.5 * swiglu(rms_norm(x, ffn_norm_g[i, 1]), ffn_w_in[i, 1], ffn_w_out[i, 1])
    return x, rwkv_S, rwkv_shift, k_new, v_new


def setup_inputs(seed: int = 0) -> dict:
    key = jax.random.key(seed)
    ks = jax.random.split(key, 32)
    f32 = jnp.float32
    D, H, N = D_MODEL, N_HEADS, HEAD_DIM
    nrm = lambda k, shape, s: jax.random.normal(k, shape, f32) * s
    return {
        "x_prompt": nrm(ks[0], (BATCH, SEQ, D), 1.0),
        "x_sample": nrm(ks[1], (DEC_BATCH, DEC_SEQ, D), 1.0),
        "state_rwkv_wkv": nrm(ks[2], (DEC_BATCH, H, N, N), 0.5),
        "state_rwkv_shift": nrm(ks[3], (DEC_BATCH, D), 1.0),
        "cache_sb_k": nrm(ks[4], (DEC_BATCH, H, PAST_LEN, N), 1.0),
        "cache_sb_v": nrm(ks[5], (DEC_BATCH, H, PAST_LEN, N), 1.0),
        "meta_tokens": nrm(ks[6], (N_META, D), 1.0),
        "ffn_norm_g": 1.0 + nrm(ks[7], (DEPTH, 2, D), 0.02),
        "ffn_w_in": nrm(ks[8], (DEPTH, 2, D, 2 * D_FF), D ** -0.5),
        "ffn_w_out": nrm(ks[9], (DEPTH, 2, D_FF, D), D_FF ** -0.5),
        "mix_norm_g": 1.0 + nrm(ks[10], (DEPTH, D), 0.02),
        "rwkv_mu": jax.random.uniform(ks[11], (6, D), f32),
        "rwkv_w_rkv": nrm(ks[12], (3, D, D), D ** -0.5),
        "rwkv_w0": jnp.linspace(-6.0, -1.0, D, dtype=f32) + nrm(ks[13], (D,), 0.1),
        "rwkv_w1": nrm(ks[14], (D, LORA_DECAY), D ** -0.5),
        "rwkv_w2": nrm(ks[15], (LORA_DECAY, D), 0.1 * LORA_DECAY ** -0.5),
        "rwkv_a0": nrm(ks[16], (D,), 0.1),
        "rwkv_a1": nrm(ks[17], (D, LORA_AAA), D ** -0.5),
        "rwkv_a2": nrm(ks[18], (LORA_AAA, D), 0.1 * LORA_AAA ** -0.5),
        "rwkv_g1": nrm(ks[19], (D, LORA_GATE), D ** -0.5),
        "rwkv_g2": nrm(ks[20], (LORA_GATE, D), LORA_GATE ** -0.5),
        "rwkv_k_k": 0.85 + nrm(ks[21], (D,), 0.02),
        "rwkv_k_a": 1.0 + nrm(ks[22], (D,), 0.02),
        "rwkv_r_k": nrm(ks[23], (H, N), 0.1),
        "rwkv_gn_w": 1.0 + nrm(ks[24], (D,), 0.02),
        "rwkv_gn_b": nrm(ks[25], (D,), 0.02),
        "rwkv_w_o": nrm(ks[26], (D, D), D ** -0.5),
        "sb_w_qkv": nrm(ks[27], (D, 3 * D), D ** -0.5),
        "sb_q_norm_g": 1.0 + nrm(ks[28], (N,), 0.02),
        "sb_k_norm_g": 1.0 + nrm(ks[29], (N,), 0.02),
        "sb_w_o": nrm(ks[30], (D, D), D ** -0.5),
    }


def reference(x_prompt, x_sample, state_rwkv_wkv, state_rwkv_shift, cache_sb_k, cache_sb_v,
              meta_tokens, ffn_norm_g, ffn_w_in, ffn_w_out, mix_norm_g,
              rwkv_mu, rwkv_w_rkv, rwkv_w0, rwkv_w1, rwkv_w2, rwkv_a0, rwkv_a1, rwkv_a2,
              rwkv_g1, rwkv_g2, rwkv_k_k, rwkv_k_a, rwkv_r_k, rwkv_gn_w, rwkv_gn_b, rwkv_w_o,
              sb_w_qkv, sb_q_norm_g, sb_k_norm_g, sb_w_o):
    rwkv_params = (rwkv_mu, rwkv_w_rkv, rwkv_w0, rwkv_w1, rwkv_w2, rwkv_a0, rwkv_a1, rwkv_a2,
                   rwkv_g1, rwkv_g2, rwkv_k_k, rwkv_k_a, rwkv_r_k, rwkv_gn_w, rwkv_gn_b, rwkv_w_o)
    sb_params = (sb_w_qkv, sb_q_norm_g, sb_k_norm_g, sb_w_o)

    B = x_prompt.shape[0]
    meta = jnp.broadcast_to(meta_tokens.astype(x_prompt.dtype)[None], (B, N_META, D_MODEL))
    xp = jnp.concatenate([meta, x_prompt], axis=1)
    S0_p = jnp.zeros((B, N_HEADS, HEAD_DIM, HEAD_DIM), x_prompt.dtype)
    shift0_p = jnp.zeros((B, D_MODEL), x_prompt.dtype)
    yp, S_p, shift_p, k_p, v_p = layer_stack(xp, S0_p, shift0_p, None, None, ffn_norm_g, ffn_w_in,
                                             ffn_w_out, mix_norm_g, rwkv_params, sb_params)
    y_prompt = yp[:, N_META:]

    ys, S_s, shift_s, k_s, v_s = layer_stack(x_sample, state_rwkv_wkv, state_rwkv_shift, cache_sb_k,
                                             cache_sb_v, ffn_norm_g, ffn_w_in, ffn_w_out, mix_norm_g,
                                             rwkv_params, sb_params)
    return (y_prompt, ys, S_p, shift_p, k_p, v_p, S_s, shift_s, k_s, v_s)
```

```python
import functools

import jax
import jax.numpy as jnp
from jax import lax
from jax.experimental import pallas as pl
from jax.experimental.pallas import tpu as pltpu

D_MODEL = 1024
HEAD_DIM = 64
N_HEADS = D_MODEL // HEAD_DIM
N_META = 16
RMS_EPS = 1e-6
GN_EPS = 64e-5

LANES = 128
CHUNK = 64
FF_CHUNK = 256
DEAD_LOG = -104.0
VMEM_LIMIT = 52 * 1024 * 1024

f32 = jnp.float32
bf16 = jnp.bfloat16

_NN = (((1,), (0,)), ((), ()))
_NT = (((1,), (1,)), ((), ()))
_TN = (((0,), (0,)), ((), ()))


def _mm(a, b, dims=_NN):
    return lax.dot_general(a.astype(bf16), b.astype(bf16), dims, preferred_element_type=f32)


def _split(a):
    hi = a.astype(bf16)
    lo = (a - hi.astype(f32)).astype(bf16)
    return hi, lo


def _mm3(a, b, dims=_NN):
    ah, al = _split(a)
    bh, bl = _split(b)
    dot = functools.partial(lax.dot_general, dimension_numbers=dims, preferred_element_type=f32)
    return dot(ah, bh) + (dot(ah, bl) + dot(al, bh))


def _mm_exact_rhs(a, b_bf16, dims=_NN):
    ah, al = _split(a)
    dot = functools.partial(lax.dot_general, dimension_numbers=dims, preferred_element_type=f32)
    return dot(ah, b_bf16) + dot(al, b_bf16)


def _mm_exact_lhs(a_bf16, b, dims=_NN):
    bh, bl = _split(b)
    dot = functools.partial(lax.dot_general, dimension_numbers=dims, preferred_element_type=f32)
    return dot(a_bf16, bh) + dot(a_bf16, bl)


def _rms(x, g):
    return x * lax.rsqrt(jnp.mean(x * x, axis=-1, keepdims=True) + RMS_EPS) * g


def _softplus(y):
    return jnp.maximum(y, 0.0) + jnp.log1p(jnp.exp(-jnp.abs(y)))


def _sigmoid(y):
    return 1.0 / (1.0 + jnp.exp(-y))


def _head_sum(x, eseg_ref, eexp_ref):
    return _mm_exact_rhs(x, eseg_ref[...])


def _head_expand(y, eexp_ref):
    return _mm_exact_rhs(y, eexp_ref[...])


def _full(shape):
    return pl.BlockSpec(shape, lambda *_: (0,) * len(shape))


def _resident(shape):
    return pl.BlockSpec(shape, lambda *_: (0,) * len(shape), pipeline_mode=pl.Buffered(1))


def _params(sem):
    return pltpu.CompilerParams(dimension_semantics=sem, vmem_limit_bytes=VMEM_LIMIT)


def _ffn_kernel(x_ref, g_ref, wg_ref, wu_ref, wo_ref, o_ref, acc_ref):
    x = x_ref[...]
    h = _rms(x, g_ref[...]).astype(bf16)
    n_chunks = wg_ref.shape[1] // FF_CHUNK
    for c in range(n_chunks):
        sl = slice(c * FF_CHUNK, (c + 1) * FF_CHUNK)
        gate = jnp.dot(h, wg_ref[:, sl], preferred_element_type=f32)
        up = jnp.dot(h, wu_ref[:, sl], preferred_element_type=f32)
        act = (gate * _sigmoid(gate) * up).astype(bf16)
        part = jnp.dot(act, wo_ref[sl, :], preferred_element_type=f32)
        if c == 0:
            acc_ref[...] = part
        else:
            acc_ref[...] += part
    o_ref[...] = x + 0.5 * acc_ref[...]


def _ffn(x2d, g, wg, wu, wo, tm):
    m, d = x2d.shape
    fp = wg.shape[1]
    return pl.pallas_call(
        _ffn_kernel,
        grid=(m // tm,),
        in_specs=[
            pl.BlockSpec((tm, d), lambda i: (i, 0)),
            _full((1, d)),
            _resident((d, fp)),
            _resident((d, fp)),
            _resident((fp, d)),
        ],
        out_specs=pl.BlockSpec((tm, d), lambda i: (i, 0)),
        out_shape=jax.ShapeDtypeStruct((m, d), f32),
        scratch_shapes=[pltpu.VMEM((tm, d), f32)],
        compiler_params=_params(("parallel",)),
        name="ffn",
    )(x2d, g, wg, wu, wo)


def _proj_kernel(gated, *refs):
    if gated:
        y_ref, g_ref, x_ref, w_ref, o_ref = refs
        y = y_ref[...] * g_ref[...]
    else:
        y_ref, x_ref, w_ref, o_ref = refs
        y = y_ref[...]
    o_ref[...] = x_ref[...] + jnp.dot(y.astype(bf16), w_ref[...], preferred_element_type=f32)


def _proj_out(y2d, gate2d, x2d, w, tm):
    m, d = x2d.shape
    row = pl.BlockSpec((tm, d), lambda i: (i, 0))
    gated = gate2d is not None
    ins = [y2d, gate2d, x2d, w] if gated else [y2d, x2d, w]
    specs = [row] * (len(ins) - 1) + [_resident((d, d))]
    return pl.pallas_call(
        functools.partial(_proj_kernel, gated),
        grid=(m // tm,),
        in_specs=specs,
        out_specs=row,
        out_shape=jax.ShapeDtypeStruct((m, d), f32),
        compiler_params=_params(("parallel",)),
        name="proj_out",
    )(*ins)


def _rwkv_pre_kernel(x_ref, shift_ref, ng_ref, mu_ref, wrkv_ref, w0_ref, w1_ref, w2_ref,
                     a0_ref, a1_ref, a2_ref, g1_ref, g2_ref, kk_ref, ka_ref, eseg_ref, eexp_ref,
                     r_out, lw_out, k_out, v_out, kk_out, b_out, g_out, shift_out, prev_ref):
    i = pl.program_id(1)
    tt = x_ref.shape[1]

    @pl.when(i == 0)
    def _():
        prev_ref[...] = shift_ref[0]

    h = _rms(x_ref[0], ng_ref[...])
    first_row = lax.broadcasted_iota(jnp.int32, h.shape, 0) == 0
    h_prev = jnp.where(first_row, prev_ref[...], pltpu.roll(h, 1, axis=0))
    last = h[tt - 1:tt, :]
    prev_ref[...] = last
    shift_out[0] = last

    dx = h_prev - h
    xr, xw, xk, xv, xa, xg = ((h + dx * mu_ref[j:j + 1, :]).astype(bf16) for j in range(6))
    r = jnp.dot(xr, wrkv_ref[0], preferred_element_type=f32)
    k = jnp.dot(xk, wrkv_ref[1], preferred_element_type=f32)
    v = jnp.dot(xv, wrkv_ref[2], preferred_element_type=f32)
    lora_w = _mm(jnp.tanh(jnp.dot(xw, w1_ref[...], preferred_element_type=f32)), w2_ref[...])
    log_rate = -_softplus(-(w0_ref[...] + lora_w)) - 0.5
    a = _sigmoid(a0_ref[...] + _mm(jnp.dot(xa, a1_ref[...], preferred_element_type=f32), a2_ref[...]))
    g = _mm(_sigmoid(jnp.dot(xg, g1_ref[...], preferred_element_type=f32)), g2_ref[...])

    kk = k * kk_ref[...]
    norm = jnp.sqrt(_head_sum(kk * kk, eseg_ref, eexp_ref))
    kk = kk * _head_expand(1.0 / jnp.maximum(norm, 1e-12), eexp_ref)

    r_out[0] = r
    lw_out[0] = -jnp.exp(log_rate)
    k_out[0] = k * (1.0 + (a - 1.0) * ka_ref[...])
    v_out[0] = v
    kk_out[0] = kk
    b_out[0] = kk * a
    g_out[0] = g


def _rwkv_pre(x, shift0, ng, p, eseg, eexp, tt):
    b, t, d = x.shape
    seq = pl.BlockSpec((1, tt, d), lambda bi, i: (bi, i, 0))
    vec = pl.BlockSpec((1, 1, d), lambda bi, i: (bi, 0, 0))
    out = jax.ShapeDtypeStruct((b, t, d), f32)
    ins = [x, shift0, ng, p["mu"], p["w_rkv"], p["w0"], p["w1"], p["w2"], p["a0"], p["a1"], p["a2"],
           p["g1"], p["g2"], p["k_k"], p["k_a"], eseg, eexp]
    specs = [seq, vec] + [_resident(a.shape) for a in ins[2:]]
    return pl.pallas_call(
        _rwkv_pre_kernel,
        grid=(b, t // tt),
        in_specs=specs,
        out_specs=[seq] * 7 + [vec],
        out_shape=[out] * 7 + [jax.ShapeDtypeStruct((b, 1, d), f32)],
        scratch_shapes=[pltpu.VMEM((1, d), f32)],
        compiler_params=_params(("parallel", "arbitrary")),
        name="rwkv_pre",
    )(*ins)


def _rwkv_scan_kernel(t_total, r_ref, lw_ref, k_ref, v_ref, kk_ref, b_ref, s0_ref, rk_ref, gnw_ref, gnb_ref,
                      tri_ref, y_ref, s_ref):
    c = pl.program_id(1)
    C, N = CHUNK, HEAD_DIM

    @pl.when(c == 0)
    def _():
        s_ref[...] = s0_ref[...]

    r, lw, k, v, kk, b = (ref[0] for ref in (r_ref, lw_ref, k_ref, v_ref, kk_ref, b_ref))
    if t_total % C:
        valid = c * C + lax.broadcasted_iota(jnp.int32, r.shape, 0) < t_total
        r, lw, k, v, kk, b = (jnp.where(valid, a, 0.0) for a in (r, lw, k, v, kk, b))

    cum = _mm_exact_lhs(tri_ref[...], lw)
    p_t = jnp.exp(cum)
    p_inv = jnp.exp(-cum)
    r_t = r * p_t
    a_t = -kk * jnp.exp(cum - lw)
    b_t = b * p_inv
    k_t = k * p_inv
    gamma = p_t[C - 1:C, :]
    b_g = b_t * gamma
    k_g = k_t * gamma
    bonus_src = r * k * rk_ref[...]

    row = lax.broadcasted_iota(jnp.int32, (C, C), 0)
    col = lax.broadcasted_iota(jnp.int32, (C, C), 1)
    strict = row > col
    incl = row >= col
    eye = (row == col).astype(f32)

    outs = []
    for hd in range(N_HEADS):
        sl = slice(hd * N, (hd + 1) * N)
        ar = jnp.concatenate([a_t[:, sl], r_t[:, sl]], axis=0)
        xb = _mm(ar, b_t[:, sl], _NT)
        xk = _mm(ar, k_t[:, sl], _NT)
        low = jnp.where(strict, xb[:C], 0.0)
        a_ak = jnp.where(strict, xk[:C], 0.0)
        a_rb = jnp.where(incl, xb[C:], 0.0)
        a_rk = jnp.where(incl, xk[C:], 0.0)
        t_inv = eye + low
        pw = low
        for _ in range(5):
            pw = _mm(pw, pw)
            t_inv = t_inv + _mm(pw, t_inv)
        s_h = s_ref[0, hd]
        v_h = v[:, sl]
        xs = _mm3(ar, s_h, _NT)
        u = _mm(t_inv, xs[:C] + _mm(a_ak, v_h))
        o = xs[C:] + _mm(a_rb, u) + _mm(a_rk, v_h)
        s_ref[0, hd] = s_h * gamma[:, sl] + _mm(u, b_g[:, sl], _TN) + _mm(v_h, k_g[:, sl], _TN)

        mean = jnp.mean(o, axis=-1, keepdims=True)
        var = jnp.mean(jnp.square(o - mean), axis=-1, keepdims=True)
        o = (o - mean) * lax.rsqrt(var + GN_EPS) * gnw_ref[:, sl] + gnb_ref[:, sl]
        outs.append(o + jnp.sum(bonus_src[:, sl], axis=-1, keepdims=True) * v_h)
    y_ref[0] = jnp.concatenate(outs, axis=1)


def _rwkv_scan(r, lw, k, v, kk, b, s0, p, tri):
    bsz, t, d = r.shape
    nc = pl.cdiv(t, CHUNK)
    seq = pl.BlockSpec((1, CHUNK, d), lambda bi, c: (bi, c, 0))
    st = pl.BlockSpec((1, N_HEADS, HEAD_DIM, HEAD_DIM), lambda bi, c: (bi, 0, 0, 0))
    consts = [p["r_k"], p["gn_w"], p["gn_b"], tri]
    return pl.pallas_call(
        functools.partial(_rwkv_scan_kernel, t),
        grid=(bsz, nc),
        in_specs=[seq] * 6 + [st] + [_resident(a.shape) for a in consts],
        out_specs=[seq, st],
        out_shape=[jax.ShapeDtypeStruct((bsz, t, d), f32), jax.ShapeDtypeStruct(s0.shape, f32)],
        compiler_params=_params(("parallel", "arbitrary")),
        name="rwkv_scan",
    )(r, lw, k, v, kk, b, s0, *consts)


def _sb_pre_kernel(x_ref, ng_ref, w_ref, qg_ref, kg_ref, eseg_ref, eexp_ref, q_out, k_out, v_out):
    d = x_ref.shape[1]
    h = _rms(x_ref[...], ng_ref[...]).astype(bf16)

    def head_rms(y, g):
        ms = _head_sum(y * y, eseg_ref, eexp_ref) * (1.0 / HEAD_DIM)
        return y * _head_expand(lax.rsqrt(ms + RMS_EPS), eexp_ref) * g

    q_out[...] = head_rms(jnp.dot(h, w_ref[:, :d], preferred_element_type=f32), qg_ref[...])
    k_out[...] = head_rms(jnp.dot(h, w_ref[:, d:2 * d], preferred_element_type=f32), kg_ref[...])
    v_out[...] = jnp.dot(h, w_ref[:, 2 * d:], preferred_element_type=f32)


def _sb_pre(x2d, ng, w_qkv, qg, kg, eseg, eexp, tm):
    m, d = x2d.shape
    row = pl.BlockSpec((tm, d), lambda i: (i, 0))
    ins = [x2d, ng, w_qkv, qg, kg, eseg, eexp]
    out = jax.ShapeDtypeStruct((m, d), f32)
    return pl.pallas_call(
        _sb_pre_kernel,
        grid=(m // tm,),
        in_specs=[row] + [_resident(a.shape) for a in ins[1:]],
        out_specs=[row] * 3,
        out_shape=[out] * 3,
        compiler_params=_params(("parallel",)),
        name="sb_pre",
    )(*ins)


def _sb_block(q, k_blk, v_blk, tri_ref, carry, vis):
    z = _mm3(q, k_blk, _NT) * (HEAD_DIM ** -0.5)
    log_beta = jnp.minimum(z, 0.0) - jnp.log1p(jnp.exp(-jnp.abs(z)))
    log_1m = log_beta - z
    if vis is not None:
        log_1m = jnp.where(vis, log_1m, 0.0)
    later = _mm_exact_rhs(log_1m, tri_ref[...])
    a = jnp.exp(log_beta + later + carry)
    if vis is not None:
        a = jnp.where(vis, a, 0.0)
    return _mm(a, v_blk), carry + later[:, :1] + log_1m[:, :1]


def _sb_prompt_kernel(q_ref, k_ref, v_ref, tri_ref, o_ref):
    qi = pl.program_id(2)
    tq = q_ref.shape[1]
    N = HEAD_DIM
    row = lax.broadcasted_iota(jnp.int32, (tq, tq), 0)
    col = lax.broadcasted_iota(jnp.int32, (tq, tq), 1)
    causal = col < row
    outs = []
    for s in range(LANES // N):
        sl = slice(s * N, (s + 1) * N)
        q = q_ref[0, :, sl]
        acc, carry = _sb_block(q, k_ref[0, pl.ds(pl.multiple_of(qi * tq, 8), tq), sl],
                               v_ref[0, pl.ds(pl.multiple_of(qi * tq, 8), tq), sl], tri_ref,
                               jnp.zeros((tq, 1), f32), causal)

        def cond(st):
            j, _, carry = st
            return jnp.logical_and(j >= 0, jnp.max(carry) > DEAD_LOG)

        def body(st):
            j, acc, carry = st
            start = pl.multiple_of(j * tq, 8)
            part, carry = _sb_block(q, k_ref[0, pl.ds(start, tq), sl], v_ref[0, pl.ds(start, tq), sl],
                                    tri_ref, carry, None)
            return j - 1, acc + part, carry

        _, acc, _ = lax.while_loop(cond, body, (qi - 1, acc, carry))
        outs.append(acc)
    o_ref[0] = jnp.concatenate(outs, axis=1)


def _sb_prompt(q, k, v, tri, tq):
    b, t, d = q.shape
    qspec = pl.BlockSpec((1, tq, LANES), lambda bi, hp, qi: (bi, qi, hp))
    kvspec = pl.BlockSpec((1, t, LANES), lambda bi, hp, qi: (bi, 0, hp))
    return pl.pallas_call(
        _sb_prompt_kernel,
        grid=(b, d // LANES, t // tq),
        in_specs=[qspec, kvspec, kvspec, _resident(tri.shape)],
        out_specs=qspec,
        out_shape=jax.ShapeDtypeStruct((b, t, d), f32),
        compiler_params=_params(("parallel", "parallel", "arbitrary")),
        name="sb_prompt",
    )(q, k, v, tri)


def _sb_sample_kernel(q_ref, k_ref, v_ref, kp_ref, vp_ref, tri_new_ref, tri_ref, o_ref, acc_ref, carry_ref):
    kb = pl.program_id(2)
    nkb = pl.num_programs(2)
    hg = q_ref.shape[1]
    t = q_ref.shape[2]
    row = lax.broadcasted_iota(jnp.int32, (t, t), 0)
    col = lax.broadcasted_iota(jnp.int32, (t, t), 1)
    causal = col < row
    for hd in range(hg):
        q = q_ref[0, hd]

        @pl.when(kb == 0)
        def _():
            acc, carry = _sb_block(q, k_ref[0, hd], v_ref[0, hd], tri_new_ref, jnp.zeros((t, 1), f32), causal)
            acc_ref[hd] = acc
            carry_ref[hd] = carry

        @pl.when(jnp.max(carry_ref[hd]) > DEAD_LOG)
        def _():
            part, carry = _sb_block(q, kp_ref[0, hd], vp_ref[0, hd], tri_ref, carry_ref[hd], None)
            acc_ref[hd] += part
            carry_ref[hd] = carry

        @pl.when(kb == nkb - 1)
        def _():
            o_ref[0, hd] = acc_ref[hd]


def _sb_sample(q, k, v, k_past, v_past, tri_new, tri, hg, kblk):
    b, h, t, n = q.shape
    p = k_past.shape[2]
    nkb = p // kblk
    new = pl.BlockSpec((1, hg, t, n), lambda bi, g, kb: (bi, g, 0, 0))
    past = pl.BlockSpec((1, hg, kblk, n), lambda bi, g, kb: (bi, g, nkb - 1 - kb, 0))
    return pl.pallas_call(
        _sb_sample_kernel,
        grid=(b, h // hg, nkb),
        in_specs=[new, new, new, past, past, _resident(tri_new.shape), _resident(tri.shape)],
        out_specs=new,
        out_shape=jax.ShapeDtypeStruct((b, h, t, n), f32),
        scratch_shapes=[pltpu.VMEM((hg, t, n), f32), pltpu.VMEM((hg, t, 1), f32)],
        compiler_params=_params(("parallel", "parallel", "arbitrary")),
        name="sb_sample",
    )(q, k, v, k_past, v_past, tri_new, tri)


def _later_matrix(n):
    i = jnp.arange(n)
    return (i[:, None] > i[None, :]).astype(bf16)


def _to_heads(x):
    b, t, _ = x.shape
    return x.reshape(b, t, N_HEADS, HEAD_DIM).transpose(0, 2, 1, 3)


def _from_heads(x):
    b, h, t, n = x.shape
    return x.transpose(0, 2, 1, 3).reshape(b, t, h * n)


def _stack(x, s0, shift0, k_past, v_past, w, tm, tt, tq):
    b, t, d = x.shape
    m = b * t
    ffn = lambda x2, i, j: _ffn(x2, w["ffn_g"][i][j], w["ffn_wg"][i][j], w["ffn_wu"][i][j], w["ffn_wo"][i][j], tm)
    x2 = x.reshape(m, d)

    x2 = ffn(x2, 0, 0)
    rw = w["rwkv"]
    r, lw, k, v, kk, bb, g, shift = _rwkv_pre(x2.reshape(b, t, d), shift0.reshape(b, 1, d), w["mix_g"][0], rw,
                                              w["eseg"], w["eexp"], tt)
    y, s_fin = _rwkv_scan(r, lw, k, v, kk, bb, s0, rw, w["tri_chunk"])
    x2 = _proj_out(y.reshape(m, d), g.reshape(m, d), x2, rw["w_o"], tm)
    x2 = ffn(x2, 0, 1)

    x2 = ffn(x2, 1, 0)
    sb = w["sb"]
    q, k, v = _sb_pre(x2, w["mix_g"][1], sb["w_qkv"], sb["q_g"], sb["k_g"], w["eseg"], w["eexp"], tm)
    k_heads = _to_heads(k.reshape(b, t, d))
    v_heads = _to_heads(v.reshape(b, t, d))
    if k_past is None:
        o = _sb_prompt(q.reshape(b, t, d), k.reshape(b, t, d), v.reshape(b, t, d), _later_matrix(tq), tq)
    else:
        kblk = 512
        o = _from_heads(_sb_sample(_to_heads(q.reshape(b, t, d)), k_heads, v_heads, k_past, v_past,
                                   _later_matrix(t), _later_matrix(kblk), 4, kblk))
    x2 = _proj_out(o.reshape(m, d), None, x2, sb["w_o"], tm)
    x2 = ffn(x2, 1, 1)
    return x2.reshape(b, t, d), s_fin, shift.reshape(b, d), k_heads, v_heads


def kernel(x_prompt, x_sample, state_rwkv_wkv, state_rwkv_shift, cache_sb_k, cache_sb_v, meta_tokens,
           ffn_norm_g, ffn_w_in, ffn_w_out, mix_norm_g, rwkv_mu, rwkv_w_rkv, rwkv_w0, rwkv_w1, rwkv_w2,
           rwkv_a0, rwkv_a1, rwkv_a2, rwkv_g1, rwkv_g2, rwkv_k_k, rwkv_k_a, rwkv_r_k, rwkv_gn_w, rwkv_gn_b,
           rwkv_w_o, sb_w_qkv, sb_q_norm_g, sb_k_norm_g, sb_w_o):
    d = D_MODEL
    depth = ffn_w_in.shape[0]
    d_ff = ffn_w_out.shape[2]
    pad = (-d_ff) % FF_CHUNK
    w_gate = jnp.pad(ffn_w_in[..., :d_ff], ((0, 0), (0, 0), (0, 0), (0, pad))).astype(bf16)
    w_up = jnp.pad(ffn_w_in[..., d_ff:], ((0, 0), (0, 0), (0, 0), (0, pad))).astype(bf16)
    w_out = jnp.pad(ffn_w_out, ((0, 0), (0, 0), (0, pad), (0, 0))).astype(bf16)
    vec = lambda a: a.reshape(1, d).astype(f32)
    head_of_lane = jnp.arange(d) // HEAD_DIM
    eseg = (head_of_lane[:, None] == jnp.arange(LANES)[None, :]).astype(bf16)
    w = {
        "ffn_g": [[vec(ffn_norm_g[i, j]) for j in range(2)] for i in range(depth)],
        "ffn_wg": w_gate, "ffn_wu": w_up, "ffn_wo": w_out,
        "mix_g": [vec(mix_norm_g[i]) for i in range(depth)],
        "eseg": eseg, "eexp": eseg.T,
        "tri_chunk": (jnp.arange(CHUNK)[:, None] >= jnp.arange(CHUNK)[None, :]).astype(bf16),
        "rwkv": {
            "mu": rwkv_mu, "w_rkv": rwkv_w_rkv.astype(bf16), "w0": vec(rwkv_w0), "w1": rwkv_w1.astype(bf16),
            "w2": rwkv_w2.astype(bf16), "a0": vec(rwkv_a0), "a1": rwkv_a1.astype(bf16), "a2": rwkv_a2.astype(bf16),
            "g1": rwkv_g1.astype(bf16), "g2": rwkv_g2.astype(bf16), "k_k": vec(rwkv_k_k), "k_a": vec(rwkv_k_a),
            "r_k": vec(rwkv_r_k), "gn_w": vec(rwkv_gn_w), "gn_b": vec(rwkv_gn_b), "w_o": rwkv_w_o.astype(bf16),
        },
        "sb": {
            "w_qkv": sb_w_qkv.astype(bf16), "q_g": vec(jnp.tile(sb_q_norm_g, N_HEADS)),
            "k_g": vec(jnp.tile(sb_k_norm_g, N_HEADS)), "w_o": sb_w_o.astype(bf16),
        },
    }

    bp = x_prompt.shape[0]
    meta = jnp.broadcast_to(meta_tokens.astype(x_prompt.dtype)[None], (bp, N_META, d))
    xp = jnp.concatenate([meta, x_prompt], axis=1)
    s0_p = jnp.zeros((bp, N_HEADS, HEAD_DIM, HEAD_DIM), f32)
    shift0_p = jnp.zeros((bp, d), f32)
    yp, s_p, shift_p, k_p, v_p = _stack(xp, s0_p, shift0_p, None, None, w, tm=688, tt=344, tq=344)
    ys, s_s, shift_s, k_s, v_s = _stack(x_sample, state_rwkv_wkv, state_rwkv_shift, cache_sb_k, cache_sb_v, w,
                                        tm=512, tt=64, tq=64)
    return (yp[:, N_META:], ys, s_p, shift_p, k_p, v_p, s_s, shift_s, k_s, v_s)
```

```python
import functools

import jax
import jax.numpy as jnp
from jax import lax
from jax.experimental import pallas as pl
from jax.experimental.pallas import tpu as pltpu

D_MODEL = 1024
HEAD_DIM = 64
N_HEADS = D_MODEL // HEAD_DIM
N_META = 16
RMS_EPS = 1e-6
GN_EPS = 64e-5

LANES = 128
CHUNK = 64
FF_CHUNK = 256
DEAD_LOG = -104.0
VMEM_LIMIT = 52 * 1024 * 1024

f32 = jnp.float32
bf16 = jnp.bfloat16

_NN = (((1,), (0,)), ((), ()))
_NT = (((1,), (1,)), ((), ()))
_TN = (((0,), (0,)), ((), ()))


def _mm(a, b, dims=_NN):
    return lax.dot_general(a.astype(bf16), b.astype(bf16), dims, preferred_element_type=f32)


def _split(a):
    hi = a.astype(bf16)
    lo = (a - hi.astype(f32)).astype(bf16)
    return hi, lo


def _mm3(a, b, dims=_NN):
    ah, al = _split(a)
    bh, bl = _split(b)
    dot = functools.partial(lax.dot_general, dimension_numbers=dims, preferred_element_type=f32)
    return dot(ah, bh) + (dot(ah, bl) + dot(al, bh))


def _mm_exact_rhs(a, b_bf16, dims=_NN):
    ah, al = _split(a)
    dot = functools.partial(lax.dot_general, dimension_numbers=dims, preferred_element_type=f32)
    return dot(ah, b_bf16) + dot(al, b_bf16)


def _mm_exact_lhs(a_bf16, b, dims=_NN):
    bh, bl = _split(b)
    dot = functools.partial(lax.dot_general, dimension_numbers=dims, preferred_element_type=f32)
    return dot(a_bf16, bh) + dot(a_bf16, bl)


def _rms(x, g):
    return x * lax.rsqrt(jnp.mean(x * x, axis=-1, keepdims=True) + RMS_EPS) * g


def _softplus(y):
    return jnp.maximum(y, 0.0) + jnp.log1p(jnp.exp(-jnp.abs(y)))


def _sigmoid(y):
    return 1.0 / (1.0 + jnp.exp(-y))


def _head_sum(x, eseg_ref, eexp_ref):
    return _mm_exact_rhs(x, eseg_ref[...])


def _head_expand(y, eexp_ref):
    return _mm_exact_rhs(y, eexp_ref[...])


def _full(shape):
    return pl.BlockSpec(shape, lambda *_: (0,) * len(shape))


def _resident(shape):
    return pl.BlockSpec(shape, lambda *_: (0,) * len(shape), pipeline_mode=pl.Buffered(1))


def _params(sem):
    return pltpu.CompilerParams(dimension_semantics=sem, vmem_limit_bytes=VMEM_LIMIT)


def _ffn_kernel(x_ref, g_ref, wg_ref, wu_ref, wo_ref, o_ref, acc_ref):
    x = x_ref[...]
    h = _rms(x, g_ref[...]).astype(bf16)
    n_chunks = wg_ref.shape[1] // FF_CHUNK
    for c in range(n_chunks):
        sl = slice(c * FF_CHUNK, (c + 1) * FF_CHUNK)
        gate = jnp.dot(h, wg_ref[:, sl], preferred_element_type=f32)
        up = jnp.dot(h, wu_ref[:, sl], preferred_element_type=f32)
        act = (gate * _sigmoid(gate) * up).astype(bf16)
        part = jnp.dot(act, wo_ref[sl, :], preferred_element_type=f32)
        if c == 0:
            acc_ref[...] = part
        else:
            acc_ref[...] += part
    o_ref[...] = x + 0.5 * acc_ref[...]


def _ffn(x2d, g, wg, wu, wo, tm):
    m, d = x2d.shape
    fp = wg.shape[1]
    return pl.pallas_call(
        _ffn_kernel,
        grid=(m // tm,),
        in_specs=[
            pl.BlockSpec((tm, d), lambda i: (i, 0)),
            _full((1, d)),
            _resident((d, fp)),
            _resident((d, fp)),
            _resident((fp, d)),
        ],
        out_specs=pl.BlockSpec((tm, d), lambda i: (i, 0)),
        out_shape=jax.ShapeDtypeStruct((m, d), f32),
        scratch_shapes=[pltpu.VMEM((tm, d), f32)],
        compiler_params=_params(("parallel",)),
        name="ffn",
    )(x2d, g, wg, wu, wo)


def _proj_kernel(gated, *refs):
    if gated:
        y_ref, g_ref, x_ref, w_ref, o_ref = refs
        y = y_ref[...] * g_ref[...]
    else:
        y_ref, x_ref, w_ref, o_ref = refs
        y = y_ref[...]
    o_ref[...] = x_ref[...] + jnp.dot(y.astype(bf16), w_ref[...], preferred_element_type=f32)


def _proj_out(y2d, gate2d, x2d, w, tm):
    m, d = x2d.shape
    row = pl.BlockSpec((tm, d), lambda i: (i, 0))
    gated = gate2d is not None
    ins = [y2d, gate2d, x2d, w] if gated else [y2d, x2d, w]
    specs = [row] * (len(ins) - 1) + [_resident((d, d))]
    return pl.pallas_call(
        functools.partial(_proj_kernel, gated),
        grid=(m // tm,),
        in_specs=specs,
        out_specs=row,
        out_shape=jax.ShapeDtypeStruct((m, d), f32),
        compiler_params=_params(("parallel",)),
        name="proj_out",
    )(*ins)


def _rwkv_pre_kernel(x_ref, shift_ref, ng_ref, mu_ref, wrkv_ref, w0_ref, w1_ref, w2_ref,
                     a0_ref, a1_ref, a2_ref, g1_ref, g2_ref, kk_ref, ka_ref, eseg_ref, eexp_ref,
                     r_out, lw_out, k_out, v_out, kk_out, b_out, g_out, shift_out, prev_ref):
    i = pl.program_id(1)
    tt = x_ref.shape[1]

    @pl.when(i == 0)
    def _():
        prev_ref[...] = shift_ref[0]

    h = _rms(x_ref[0], ng_ref[...])
    first_row = lax.broadcasted_iota(jnp.int32, h.shape, 0) == 0
    h_prev = jnp.where(first_row, prev_ref[...], pltpu.roll(h, 1, axis=0))
    last = h[tt - 1:tt, :]
    prev_ref[...] = last
    shift_out[0] = last

    dx = h_prev - h
    xr, xw, xk, xv, xa, xg = ((h + dx * mu_ref[j:j + 1, :]).astype(bf16) for j in range(6))
    r = jnp.dot(xr, wrkv_ref[0], preferred_element_type=f32)
    k = jnp.dot(xk, wrkv_ref[1], preferred_element_type=f32)
    v = jnp.dot(xv, wrkv_ref[2], preferred_element_type=f32)
    lora_w = _mm(jnp.tanh(jnp.dot(xw, w1_ref[...], preferred_element_type=f32)), w2_ref[...])
    log_rate = -_softplus(-(w0_ref[...] + lora_w)) - 0.5
    a = _sigmoid(a0_ref[...] + _mm(jnp.dot(xa, a1_ref[...], preferred_element_type=f32), a2_ref[...]))
    g = _mm(_sigmoid(jnp.dot(xg, g1_ref[...], preferred_element_type=f32)), g2_ref[...])

    kk = k * kk_ref[...]
    norm = jnp.sqrt(_head_sum(kk * kk, eseg_ref, eexp_ref))
    kk = kk * _head_expand(1.0 / jnp.maximum(norm, 1e-12), eexp_ref)

    r_out[0] = r
    lw_out[0] = -jnp.exp(log_rate)
    k_out[0] = k * (1.0 + (a - 1.0) * ka_ref[...])
    v_out[0] = v
    kk_out[0] = kk
    b_out[0] = kk * a
    g_out[0] = g


def _rwkv_pre(x, shift0, ng, p, eseg, eexp, tt):
    b, t, d = x.shape
    seq = pl.BlockSpec((1, tt, d), lambda bi, i: (bi, i, 0))
    vec = pl.BlockSpec((1, 1, d), lambda bi, i: (bi, 0, 0))
    out = jax.ShapeDtypeStruct((b, t, d), f32)
    ins = [x, shift0, ng, p["mu"], p["w_rkv"], p["w0"], p["w1"], p["w2"], p["a0"], p["a1"], p["a2"],
           p["g1"], p["g2"], p["k_k"], p["k_a"], eseg, eexp]
    specs = [seq, vec] + [_resident(a.shape) for a in ins[2:]]
    return pl.pallas_call(
        _rwkv_pre_kernel,
        grid=(b, t // tt),
        in_specs=specs,
        out_specs=[seq] * 7 + [vec],
        out_shape=[out] * 7 + [jax.ShapeDtypeStruct((b, 1, d), f32)],
        scratch_shapes=[pltpu.VMEM((1, d), f32)],
        compiler_params=_params(("parallel", "arbitrary")),
        name="rwkv_pre",
    )(*ins)


def _rwkv_scan_kernel(t_total, r_ref, lw_ref, k_ref, v_ref, kk_ref, b_ref, s0_ref, rk_ref, gnw_ref, gnb_ref,
                      tri_ref, y_ref, s_ref):
    c = pl.program_id(1)
    C, N = CHUNK, HEAD_DIM

    @pl.when(c == 0)
    def _():
        s_ref[...] = s0_ref[...]

    r, lw, k, v, kk, b = (ref[0] for ref in (r_ref, lw_ref, k_ref, v_ref, kk_ref, b_ref))
    if t_total % C:
        valid = c * C + lax.broadcasted_iota(jnp.int32, r.shape, 0) < t_total
        r, lw, k, v, kk, b = (jnp.where(valid, a, 0.0) for a in (r, lw, k, v, kk, b))

    cum = _mm_exact_lhs(tri_ref[...], lw)
    p_t = jnp.exp(cum)
    p_inv = jnp.exp(-cum)
    r_t = r * p_t
    a_t = -kk * jnp.exp(cum - lw)
    b_t = b * p_inv
    k_t = k * p_inv
    gamma = p_t[C - 1:C, :]
    b_g = b_t * gamma
    k_g = k_t * gamma
    bonus_src = r * k * rk_ref[...]

    row = lax.broadcasted_iota(jnp.int32, (C, C), 0)
    col = lax.broadcasted_iota(jnp.int32, (C, C), 1)
    strict = row > col
    incl = row >= col
    eye = (row == col).astype(f32)

    heads = range(N_HEADS)
    sls = [slice(hd * N, (hd + 1) * N) for hd in heads]
    s_old = [s_ref[0, hd] for hd in heads]
    v_h = [v[:, sl] for sl in sls]
    ar = [jnp.concatenate([a_t[:, sl], r_t[:, sl]], axis=0) for sl in sls]
    xb = [_mm(ar[hd], b_t[:, sls[hd]], _NT) for hd in heads]
    xk = [_mm(ar[hd], k_t[:, sls[hd]], _NT) for hd in heads]
    xs = [_mm3(ar[hd], s_old[hd], _NT) for hd in heads]
    low = [jnp.where(strict, x[:C], 0.0) for x in xb]
    a_rb = [jnp.where(incl, x[C:], 0.0) for x in xb]
    a_ak = [jnp.where(strict, x[:C], 0.0) for x in xk]
    a_rk = [jnp.where(incl, x[C:], 0.0) for x in xk]
    t_inv = [eye + x for x in low]
    pw = low
    for _ in range(5):
        pw = [_mm(x, x) for x in pw]
        t_inv = [t + _mm(x, t) for x, t in zip(pw, t_inv)]
    rhs = [xs[hd][:C] + _mm(a_ak[hd], v_h[hd]) for hd in heads]
    u = [_mm(t_inv[hd], rhs[hd]) for hd in heads]
    o = [xs[hd][C:] + _mm(a_rb[hd], u[hd]) + _mm(a_rk[hd], v_h[hd]) for hd in heads]
    for hd in heads:
        sl = sls[hd]
        s_ref[0, hd] = (s_old[hd] * gamma[:, sl] + _mm(u[hd], b_g[:, sl], _TN)
                        + _mm(v_h[hd], k_g[:, sl], _TN))
    outs = []
    for hd in heads:
        sl = sls[hd]
        mean = jnp.mean(o[hd], axis=-1, keepdims=True)
        var = jnp.mean(jnp.square(o[hd] - mean), axis=-1, keepdims=True)
        o_n = (o[hd] - mean) * lax.rsqrt(var + GN_EPS) * gnw_ref[:, sl] + gnb_ref[:, sl]
        outs.append(o_n + jnp.sum(bonus_src[:, sl], axis=-1, keepdims=True) * v_h[hd])
    y_ref[0] = jnp.concatenate(outs, axis=1)


def _rwkv_scan(r, lw, k, v, kk, b, s0, p, tri):
    bsz, t, d = r.shape
    nc = pl.cdiv(t, CHUNK)
    seq = pl.BlockSpec((1, CHUNK, d), lambda bi, c: (bi, c, 0))
    st = pl.BlockSpec((1, N_HEADS, HEAD_DIM, HEAD_DIM), lambda bi, c: (bi, 0, 0, 0))
    consts = [p["r_k"], p["gn_w"], p["gn_b"], tri]
    return pl.pallas_call(
        functools.partial(_rwkv_scan_kernel, t),
        grid=(bsz, nc),
        in_specs=[seq] * 6 + [st] + [_resident(a.shape) for a in consts],
        out_specs=[seq, st],
        out_shape=[jax.ShapeDtypeStruct((bsz, t, d), f32), jax.ShapeDtypeStruct(s0.shape, f32)],
        compiler_params=_params(("parallel", "arbitrary")),
        name="rwkv_scan",
    )(r, lw, k, v, kk, b, s0, *consts)


def _sb_pre_kernel(x_ref, ng_ref, w_ref, qg_ref, kg_ref, eseg_ref, eexp_ref, q_out, k_out, v_out, *head_outs):
    d = x_ref.shape[2]
    h = _rms(x_ref[0], ng_ref[...]).astype(bf16)

    def head_rms(y, g):
        ms = _head_sum(y * y, eseg_ref, eexp_ref) * (1.0 / HEAD_DIM)
        return y * _head_expand(lax.rsqrt(ms + RMS_EPS), eexp_ref) * g

    q = head_rms(jnp.dot(h, w_ref[:, :d], preferred_element_type=f32), qg_ref[...])
    k = head_rms(jnp.dot(h, w_ref[:, d:2 * d], preferred_element_type=f32), kg_ref[...])
    v = jnp.dot(h, w_ref[:, 2 * d:], preferred_element_type=f32)
    q_out[0] = q
    k_out[0] = k
    v_out[0] = v
    for ref, val in zip(head_outs, (k, v, q)):
        for hd in range(N_HEADS):
            ref[0, hd] = val[:, hd * HEAD_DIM:(hd + 1) * HEAD_DIM]


def _sb_pre(x, ng, w_qkv, qg, kg, eseg, eexp, tt, q_heads):
    b, t, d = x.shape
    row = pl.BlockSpec((1, tt, d), lambda bi, i: (bi, i, 0))
    hrow = pl.BlockSpec((1, N_HEADS, tt, HEAD_DIM), lambda bi, i: (bi, 0, i, 0))
    ins = [x, ng, w_qkv, qg, kg, eseg, eexp]
    out = jax.ShapeDtypeStruct((b, t, d), f32)
    hout = jax.ShapeDtypeStruct((b, N_HEADS, t, HEAD_DIM), f32)
    n_head_outs = 3 if q_heads else 2
    return pl.pallas_call(
        _sb_pre_kernel,
        grid=(b, t // tt),
        in_specs=[row] + [_resident(a.shape) for a in ins[1:]],
        out_specs=[row] * 3 + [hrow] * n_head_outs,
        out_shape=[out] * 3 + [hout] * n_head_outs,
        compiler_params=_params(("parallel", "parallel")),
        name="sb_pre",
    )(*ins)


def _sb_block(q, k_blk, v_blk, tri_ref, carry, vis):
    z = _mm3(q, k_blk, _NT) * (HEAD_DIM ** -0.5)
    log_beta = jnp.minimum(z, 0.0) - jnp.log1p(jnp.exp(-jnp.abs(z)))
    log_1m = log_beta - z
    if vis is not None:
        log_1m = jnp.where(vis, log_1m, 0.0)
    later = _mm_exact_rhs(log_1m, tri_ref[...])
    a = jnp.exp(log_beta + later + carry)
    if vis is not None:
        a = jnp.where(vis, a, 0.0)
    return _mm(a, v_blk), carry + later[:, :1] + log_1m[:, :1]


def _sb_prompt_kernel(q_ref, k_ref, v_ref, tri_ref, o_ref):
    qi = pl.program_id(2)
    tq = q_ref.shape[1]
    N = HEAD_DIM
    row = lax.broadcasted_iota(jnp.int32, (tq, tq), 0)
    col = lax.broadcasted_iota(jnp.int32, (tq, tq), 1)
    causal = col < row
    outs = []
    for s in range(LANES // N):
        sl = slice(s * N, (s + 1) * N)
        q = q_ref[0, :, sl]
        acc, carry = _sb_block(q, k_ref[0, pl.ds(pl.multiple_of(qi * tq, 8), tq), sl],
                               v_ref[0, pl.ds(pl.multiple_of(qi * tq, 8), tq), sl], tri_ref,
                               jnp.zeros((tq, 1), f32), causal)

        def cond(st):
            j, _, carry = st
            return jnp.logical_and(j >= 0, jnp.max(carry) > DEAD_LOG)

        def body(st):
            j, acc, carry = st
            start = pl.multiple_of(j * tq, 8)
            part, carry = _sb_block(q, k_ref[0, pl.ds(start, tq), sl], v_ref[0, pl.ds(start, tq), sl],
                                    tri_ref, carry, None)
            return j - 1, acc + part, carry

        _, acc, _ = lax.while_loop(cond, body, (qi - 1, acc, carry))
        outs.append(acc)
    o_ref[0] = jnp.concatenate(outs, axis=1)


def _sb_prompt(q, k, v, tri, tq):
    b, t, d = q.shape
    qspec = pl.BlockSpec((1, tq, LANES), lambda bi, hp, qi: (bi, qi, hp))
    kvspec = pl.BlockSpec((1, t, LANES), lambda bi, hp, qi: (bi, 0, hp))
    return pl.pallas_call(
        _sb_prompt_kernel,
        grid=(b, d // LANES, t // tq),
        in_specs=[qspec, kvspec, kvspec, _resident(tri.shape)],
        out_specs=qspec,
        out_shape=jax.ShapeDtypeStruct((b, t, d), f32),
        compiler_params=_params(("parallel", "parallel", "arbitrary")),
        name="sb_prompt",
    )(q, k, v, tri)


def _sb_sample_kernel(q_ref, k_ref, v_ref, kp_ref, vp_ref, tri_new_ref, tri_ref, o_ref, acc_ref, carry_ref):
    kb = pl.program_id(2)
    nkb = pl.num_programs(2)
    hg = q_ref.shape[1]
    t = q_ref.shape[2]
    row = lax.broadcasted_iota(jnp.int32, (t, t), 0)
    col = lax.broadcasted_iota(jnp.int32, (t, t), 1)
    causal = col < row
    for hd in range(hg):
        q = q_ref[0, hd]

        @pl.when(kb == 0)
        def _():
            acc, carry = _sb_block(q, k_ref[0, hd], v_ref[0, hd], tri_new_ref, jnp.zeros((t, 1), f32), causal)
            acc_ref[hd] = acc
            carry_ref[hd] = carry

        @pl.when(jnp.max(carry_ref[hd]) > DEAD_LOG)
        def _():
            part, carry = _sb_block(q, kp_ref[0, hd], vp_ref[0, hd], tri_ref, carry_ref[hd], None)
            acc_ref[hd] += part
            carry_ref[hd] = carry

        @pl.when(kb == nkb - 1)
        def _():
            o_ref[0, hd] = acc_ref[hd]


def _sb_sample(q, k, v, k_past, v_past, tri_new, tri, hg, kblk):
    b, h, t, n = q.shape
    p = k_past.shape[2]
    nkb = p // kblk
    new = pl.BlockSpec((1, hg, t, n), lambda bi, g, kb: (bi, g, 0, 0))
    past = pl.BlockSpec((1, hg, kblk, n), lambda bi, g, kb: (bi, g, nkb - 1 - kb, 0))
    return pl.pallas_call(
        _sb_sample_kernel,
        grid=(b, h // hg, nkb),
        in_specs=[new, new, new, past, past, _resident(tri_new.shape), _resident(tri.shape)],
        out_specs=new,
        out_shape=jax.ShapeDtypeStruct((b, h, t, n), f32),
        scratch_shapes=[pltpu.VMEM((hg, t, n), f32), pltpu.VMEM((hg, t, 1), f32)],
        compiler_params=_params(("parallel", "parallel", "arbitrary")),
        name="sb_sample",
    )(q, k, v, k_past, v_past, tri_new, tri)


def _later_matrix(n):
    i = jnp.arange(n)
    return (i[:, None] > i[None, :]).astype(bf16)


def _to_heads(x):
    b, t, _ = x.shape
    return x.reshape(b, t, N_HEADS, HEAD_DIM).transpose(0, 2, 1, 3)


def _from_heads(x):
    b, h, t, n = x.shape
    return x.transpose(0, 2, 1, 3).reshape(b, t, h * n)


def _stack(x, s0, shift0, k_past, v_past, w, tm, tt, tq):
    b, t, d = x.shape
    m = b * t
    ffn = lambda x2, i, j: _ffn(x2, w["ffn_g"][i][j], w["ffn_wg"][i][j], w["ffn_wu"][i][j], w["ffn_wo"][i][j], tm)
    x2 = x.reshape(m, d)

    x2 = ffn(x2, 0, 0)
    rw = w["rwkv"]
    r, lw, k, v, kk, bb, g, shift = _rwkv_pre(x2.reshape(b, t, d), shift0.reshape(b, 1, d), w["mix_g"][0], rw,
                                              w["eseg"], w["eexp"], tt)
    y, s_fin = _rwkv_scan(r, lw, k, v, kk, bb, s0, rw, w["tri_chunk"])
    x2 = _proj_out(y.reshape(m, d), g.reshape(m, d), x2, rw["w_o"], tm)
    x2 = ffn(x2, 0, 1)

    x2 = ffn(x2, 1, 0)
    sb = w["sb"]
    sb_outs = _sb_pre(x2.reshape(b, t, d), w["mix_g"][1], sb["w_qkv"], sb["q_g"], sb["k_g"], w["eseg"], w["eexp"],
                      tt, k_past is not None)
    q, k, v, k_heads, v_heads = sb_outs[:5]
    if k_past is None:
        o = _sb_prompt(q, k, v, _later_matrix(tq), tq)
    else:
        kblk = 512
        o = _from_heads(_sb_sample(sb_outs[5], k_heads, v_heads, k_past, v_past,
                                   _later_matrix(t), _later_matrix(kblk), 4, kblk))
    x2 = _proj_out(o.reshape(m, d), None, x2, sb["w_o"], tm)
    x2 = ffn(x2, 1, 1)
    return x2.reshape(b, t, d), s_fin, shift.reshape(b, d), k_heads, v_heads


def kernel(x_prompt, x_sample, state_rwkv_wkv, state_rwkv_shift, cache_sb_k, cache_sb_v, meta_tokens,
           ffn_norm_g, ffn_w_in, ffn_w_out, mix_norm_g, rwkv_mu, rwkv_w_rkv, rwkv_w0, rwkv_w1, rwkv_w2,
           rwkv_a0, rwkv_a1, rwkv_a2, rwkv_g1, rwkv_g2, rwkv_k_k, rwkv_k_a, rwkv_r_k, rwkv_gn_w, rwkv_gn_b,
           rwkv_w_o, sb_w_qkv, sb_q_norm_g, sb_k_norm_g, sb_w_o):
    d = D_MODEL
    depth = ffn_w_in.shape[0]
    d_ff = ffn_w_out.shape[2]
    pad = (-d_ff) % FF_CHUNK
    w_gate = jnp.pad(ffn_w_in[..., :d_ff], ((0, 0), (0, 0), (0, 0), (0, pad))).astype(bf16)
    w_up = jnp.pad(ffn_w_in[..., d_ff:], ((0, 0), (0, 0), (0, 0), (0, pad))).astype(bf16)
    w_out = jnp.pad(ffn_w_out, ((0, 0), (0, 0), (0, pad), (0, 0))).astype(bf16)
    vec = lambda a: a.reshape(1, d).astype(f32)
    head_of_lane = jnp.arange(d) // HEAD_DIM
    eseg = (head_of_lane[:, None] == jnp.arange(LANES)[None, :]).astype(bf16)
    w = {
        "ffn_g": [[vec(ffn_norm_g[i, j]) for j in range(2)] for i in range(depth)],
        "ffn_wg": w_gate, "ffn_wu": w_up, "ffn_wo": w_out,
        "mix_g": [vec(mix_norm_g[i]) for i in range(depth)],
        "eseg": eseg, "eexp": eseg.T,
        "tri_chunk": (jnp.arange(CHUNK)[:, None] >= jnp.arange(CHUNK)[None, :]).astype(bf16),
        "rwkv": {
            "mu": rwkv_mu, "w_rkv": rwkv_w_rkv.astype(bf16), "w0": vec(rwkv_w0), "w1": rwkv_w1.astype(bf16),
            "w2": rwkv_w2.astype(bf16), "a0": vec(rwkv_a0), "a1": rwkv_a1.astype(bf16), "a2": rwkv_a2.astype(bf16),
            "g1": rwkv_g1.astype(bf16), "g2": rwkv_g2.astype(bf16), "k_k": vec(rwkv_k_k), "k_a": vec(rwkv_k_a),
            "r_k": vec(rwkv_r_k), "gn_w": vec(rwkv_gn_w), "gn_b": vec(rwkv_gn_b), "w_o": rwkv_w_o.astype(bf16),
        },
        "sb": {
            "w_qkv": sb_w_qkv.astype(bf16), "q_g": vec(jnp.tile(sb_q_norm_g, N_HEADS)),
            "k_g": vec(jnp.tile(sb_k_norm_g, N_HEADS)), "w_o": sb_w_o.astype(bf16),
        },
    }

    bp = x_prompt.shape[0]
    meta = jnp.broadcast_to(meta_tokens.astype(x_prompt.dtype)[None], (bp, N_META, d))
    xp = jnp.concatenate([meta, x_prompt], axis=1)
    s0_p = jnp.zeros((bp, N_HEADS, HEAD_DIM, HEAD_DIM), f32)
    shift0_p = jnp.zeros((bp, d), f32)
    yp, s_p, shift_p, k_p, v_p = _stack(xp, s0_p, shift0_p, None, None, w, tm=688, tt=344, tq=344)
    ys, s_s, shift_s, k_s, v_s = _stack(x_sample, state_rwkv_wkv, state_rwkv_shift, cache_sb_k, cache_sb_v, w,
                                        tm=512, tt=64, tq=64)
    return (yp[:, N_META:], ys, s_p, shift_p, k_p, v_p, s_s, shift_s, k_s, v_s)
```

```python
import functools

import jax
import jax.numpy as jnp
from jax import lax
from jax.experimental import pallas as pl
from jax.experimental.pallas import tpu as pltpu

D_MODEL = 1024
HEAD_DIM = 64
N_HEADS = D_MODEL // HEAD_DIM
N_META = 16
RMS_EPS = 1e-6
GN_EPS = 64e-5

LANES = 128
CHUNK = 64
FF_CHUNK = 256
DEAD_LOG = -104.0
VMEM_LIMIT = 52 * 1024 * 1024

f32 = jnp.float32
bf16 = jnp.bfloat16

_NN = (((1,), (0,)), ((), ()))
_NT = (((1,), (1,)), ((), ()))
_TN = (((0,), (0,)), ((), ()))


def _mm(a, b, dims=_NN):
    return lax.dot_general(a.astype(bf16), b.astype(bf16), dims, preferred_element_type=f32)


def _split(a):
    hi = a.astype(bf16)
    lo = (a - hi.astype(f32)).astype(bf16)
    return hi, lo


def _mm3(a, b, dims=_NN):
    ah, al = _split(a)
    bh, bl = _split(b)
    dot = functools.partial(lax.dot_general, dimension_numbers=dims, preferred_element_type=f32)
    return dot(ah, bh) + (dot(ah, bl) + dot(al, bh))


def _mm_exact_rhs(a, b_bf16, dims=_NN):
    ah, al = _split(a)
    dot = functools.partial(lax.dot_general, dimension_numbers=dims, preferred_element_type=f32)
    return dot(ah, b_bf16) + dot(al, b_bf16)


def _mm_exact_lhs(a_bf16, b, dims=_NN):
    bh, bl = _split(b)
    dot = functools.partial(lax.dot_general, dimension_numbers=dims, preferred_element_type=f32)
    return dot(a_bf16, bh) + dot(a_bf16, bl)


def _rms(x, g):
    return x * lax.rsqrt(jnp.mean(x * x, axis=-1, keepdims=True) + RMS_EPS) * g


def _softplus(y):
    return jnp.maximum(y, 0.0) + jnp.log1p(jnp.exp(-jnp.abs(y)))


def _sigmoid(y):
    return 1.0 / (1.0 + jnp.exp(-y))


def _head_sum(x, eseg_ref, eexp_ref):
    return _mm_exact_rhs(x, eseg_ref[...])


def _head_expand(y, eexp_ref):
    return _mm_exact_rhs(y, eexp_ref[...])


def _full(shape):
    return pl.BlockSpec(shape, lambda *_: (0,) * len(shape))


def _resident(shape):
    return pl.BlockSpec(shape, lambda *_: (0,) * len(shape), pipeline_mode=pl.Buffered(1))


def _params(sem):
    return pltpu.CompilerParams(dimension_semantics=sem, vmem_limit_bytes=VMEM_LIMIT)


def _ffn_kernel(x_ref, g_ref, wg_ref, wu_ref, wo_ref, o_ref, acc_ref):
    x = x_ref[...]
    h = _rms(x, g_ref[...]).astype(bf16)
    n_chunks = wg_ref.shape[1] // FF_CHUNK
    for c in range(n_chunks):
        sl = slice(c * FF_CHUNK, (c + 1) * FF_CHUNK)
        gate = jnp.dot(h, wg_ref[:, sl], preferred_element_type=f32)
        up = jnp.dot(h, wu_ref[:, sl], preferred_element_type=f32)
        act = (gate * _sigmoid(gate) * up).astype(bf16)
        part = jnp.dot(act, wo_ref[sl, :], preferred_element_type=f32)
        if c == 0:
            acc_ref[...] = part
        else:
            acc_ref[...] += part
    o_ref[...] = x + 0.5 * acc_ref[...]


def _ffn(x2d, g, wg, wu, wo, tm):
    m, d = x2d.shape
    fp = wg.shape[1]
    return pl.pallas_call(
        _ffn_kernel,
        grid=(m // tm,),
        in_specs=[
            pl.BlockSpec((tm, d), lambda i: (i, 0)),
            _full((1, d)),
            _resident((d, fp)),
            _resident((d, fp)),
            _resident((fp, d)),
        ],
        out_specs=pl.BlockSpec((tm, d), lambda i: (i, 0)),
        out_shape=jax.ShapeDtypeStruct((m, d), f32),
        scratch_shapes=[pltpu.VMEM((tm, d), f32)],
        compiler_params=_params(("parallel",)),
        name="ffn",
    )(x2d, g, wg, wu, wo)


def _proj_kernel(gated, *refs):
    if gated:
        y_ref, g_ref, x_ref, w_ref, o_ref = refs
        y = y_ref[...] * g_ref[...]
    else:
        y_ref, x_ref, w_ref, o_ref = refs
        y = y_ref[...]
    o_ref[...] = x_ref[...] + jnp.dot(y.astype(bf16), w_ref[...], preferred_element_type=f32)


def _proj_out(y2d, gate2d, x2d, w, tm):
    m, d = x2d.shape
    row = pl.BlockSpec((tm, d), lambda i: (i, 0))
    gated = gate2d is not None
    ins = [y2d, gate2d, x2d, w] if gated else [y2d, x2d, w]
    specs = [row] * (len(ins) - 1) + [_resident((d, d))]
    return pl.pallas_call(
        functools.partial(_proj_kernel, gated),
        grid=(m // tm,),
        in_specs=specs,
        out_specs=row,
        out_shape=jax.ShapeDtypeStruct((m, d), f32),
        compiler_params=_params(("parallel",)),
        name="proj_out",
    )(*ins)


def _rwkv_pre_kernel(x_ref, shift_ref, ng_ref, mu_ref, wrkv_ref, w0_ref, w1_ref, w2_ref,
                     a0_ref, a1_ref, a2_ref, g1_ref, g2_ref, kk_ref, ka_ref, eseg_ref, eexp_ref,
                     r_out, lw_out, k_out, v_out, kk_out, b_out, g_out, shift_out, prev_ref):
    i = pl.program_id(1)
    tt = x_ref.shape[1]

    @pl.when(i == 0)
    def _():
        prev_ref[...] = shift_ref[0]

    h = _rms(x_ref[0], ng_ref[...])
    first_row = lax.broadcasted_iota(jnp.int32, h.shape, 0) == 0
    h_prev = jnp.where(first_row, prev_ref[...], pltpu.roll(h, 1, axis=0))
    last = h[tt - 1:tt, :]
    prev_ref[...] = last
    shift_out[0] = last

    dx = h_prev - h
    xr, xw, xk, xv, xa, xg = ((h + dx * mu_ref[j:j + 1, :]).astype(bf16) for j in range(6))
    r = jnp.dot(xr, wrkv_ref[0], preferred_element_type=f32)
    k = jnp.dot(xk, wrkv_ref[1], preferred_element_type=f32)
    v = jnp.dot(xv, wrkv_ref[2], preferred_element_type=f32)
    lora_w = _mm(jnp.tanh(jnp.dot(xw, w1_ref[...], preferred_element_type=f32)), w2_ref[...])
    log_rate = -_softplus(-(w0_ref[...] + lora_w)) - 0.5
    a = _sigmoid(a0_ref[...] + _mm(jnp.dot(xa, a1_ref[...], preferred_element_type=f32), a2_ref[...]))
    g = _mm(_sigmoid(jnp.dot(xg, g1_ref[...], preferred_element_type=f32)), g2_ref[...])

    kk = k * kk_ref[...]
    norm = jnp.sqrt(_head_sum(kk * kk, eseg_ref, eexp_ref))
    kk = kk * _head_expand(1.0 / jnp.maximum(norm, 1e-12), eexp_ref)

    r_out[0] = r
    lw_out[0] = -jnp.exp(log_rate)
    k_out[0] = k * (1.0 + (a - 1.0) * ka_ref[...])
    v_out[0] = v
    kk_out[0] = kk
    b_out[0] = kk * a
    g_out[0] = g


def _rwkv_pre(x, shift0, ng, p, eseg, eexp, tt):
    b, t, d = x.shape
    seq = pl.BlockSpec((1, tt, d), lambda bi, i: (bi, i, 0))
    vec = pl.BlockSpec((1, 1, d), lambda bi, i: (bi, 0, 0))
    out = jax.ShapeDtypeStruct((b, t, d), f32)
    ins = [x, shift0, ng, p["mu"], p["w_rkv"], p["w0"], p["w1"], p["w2"], p["a0"], p["a1"], p["a2"],
           p["g1"], p["g2"], p["k_k"], p["k_a"], eseg, eexp]
    specs = [seq, vec] + [_resident(a.shape) for a in ins[2:]]
    return pl.pallas_call(
        _rwkv_pre_kernel,
        grid=(b, t // tt),
        in_specs=specs,
        out_specs=[seq] * 7 + [vec],
        out_shape=[out] * 7 + [jax.ShapeDtypeStruct((b, 1, d), f32)],
        scratch_shapes=[pltpu.VMEM((1, d), f32)],
        compiler_params=_params(("parallel", "arbitrary")),
        name="rwkv_pre",
    )(*ins)


def _rwkv_scan_kernel(t_total, r_ref, lw_ref, k_ref, v_ref, kk_ref, b_ref, s0_ref, rk_ref, gnw_ref, gnb_ref,
                      tri_ref, y_ref, s_ref):
    c = pl.program_id(1)
    C, N = CHUNK, HEAD_DIM

    @pl.when(c == 0)
    def _():
        s_ref[...] = s0_ref[...]

    r, lw, k, v, kk, b = (ref[0] for ref in (r_ref, lw_ref, k_ref, v_ref, kk_ref, b_ref))
    if t_total % C:
        valid = c * C + lax.broadcasted_iota(jnp.int32, r.shape, 0) < t_total
        r, lw, k, v, kk, b = (jnp.where(valid, a, 0.0) for a in (r, lw, k, v, kk, b))

    cum = _mm_exact_lhs(tri_ref[...], lw)
    p_t = jnp.exp(cum)
    p_inv = jnp.exp(-cum)
    r_t = r * p_t
    a_t = -kk * jnp.exp(cum - lw)
    b_t = b * p_inv
    k_t = k * p_inv
    gamma = p_t[C - 1:C, :]
    b_g = b_t * gamma
    k_g = k_t * gamma
    bonus_src = r * k * rk_ref[...]

    row = lax.broadcasted_iota(jnp.int32, (C, C), 0)
    col = lax.broadcasted_iota(jnp.int32, (C, C), 1)
    strict = row > col
    incl = row >= col
    eye = (row == col).astype(f32)

    heads = range(N_HEADS)
    sls = [slice(hd * N, (hd + 1) * N) for hd in heads]
    s_old = [s_ref[0, hd] for hd in heads]
    v_h = [v[:, sl] for sl in sls]
    ar = [jnp.concatenate([a_t[:, sl], r_t[:, sl]], axis=0) for sl in sls]
    xb = [_mm(ar[hd], b_t[:, sls[hd]], _NT) for hd in heads]
    xk = [_mm(ar[hd], k_t[:, sls[hd]], _NT) for hd in heads]
    xs = [_mm3(ar[hd], s_old[hd], _NT) for hd in heads]
    low = [jnp.where(strict, x[:C], 0.0) for x in xb]
    a_rb = [jnp.where(incl, x[C:], 0.0) for x in xb]
    a_ak = [jnp.where(strict, x[:C], 0.0) for x in xk]
    a_rk = [jnp.where(incl, x[C:], 0.0) for x in xk]
    t_inv = [eye + x for x in low]
    pw = low
    for _ in range(5):
        pw = [_mm(x, x) for x in pw]
        t_inv = [t + _mm(x, t) for x, t in zip(pw, t_inv)]
    rhs = [xs[hd][:C] + _mm(a_ak[hd], v_h[hd]) for hd in heads]
    u = [_mm(t_inv[hd], rhs[hd]) for hd in heads]
    o = [xs[hd][C:] + _mm(a_rb[hd], u[hd]) + _mm(a_rk[hd], v_h[hd]) for hd in heads]
    for hd in heads:
        sl = sls[hd]
        s_ref[0, hd] = (s_old[hd] * gamma[:, sl] + _mm(u[hd], b_g[:, sl], _TN)
                        + _mm(v_h[hd], k_g[:, sl], _TN))
    outs = []
    for hd in heads:
        sl = sls[hd]
        mean = jnp.mean(o[hd], axis=-1, keepdims=True)
        var = jnp.mean(jnp.square(o[hd] - mean), axis=-1, keepdims=True)
        o_n = (o[hd] - mean) * lax.rsqrt(var + GN_EPS) * gnw_ref[:, sl] + gnb_ref[:, sl]
        outs.append(o_n + jnp.sum(bonus_src[:, sl], axis=-1, keepdims=True) * v_h[hd])
    y_ref[0] = jnp.concatenate(outs, axis=1)


def _rwkv_scan(r, lw, k, v, kk, b, s0, p, tri):
    bsz, t, d = r.shape
    nc = pl.cdiv(t, CHUNK)
    seq = pl.BlockSpec((1, CHUNK, d), lambda bi, c: (bi, c, 0))
    st = pl.BlockSpec((1, N_HEADS, HEAD_DIM, HEAD_DIM), lambda bi, c: (bi, 0, 0, 0))
    consts = [p["r_k"], p["gn_w"], p["gn_b"], tri]
    return pl.pallas_call(
        functools.partial(_rwkv_scan_kernel, t),
        grid=(bsz, nc),
        in_specs=[seq] * 6 + [st] + [_resident(a.shape) for a in consts],
        out_specs=[seq, st],
        out_shape=[jax.ShapeDtypeStruct((bsz, t, d), f32), jax.ShapeDtypeStruct(s0.shape, f32)],
        compiler_params=_params(("parallel", "arbitrary")),
        name="rwkv_scan",
    )(r, lw, k, v, kk, b, s0, *consts)


SB_PRE_HEADS = 4


def _sb_pre_kernel(x_ref, ng_ref, wq_ref, wkv_ref, qg_ref, kg_ref, eseg_ref, eexp_ref, q_out, kt_out, vt_out, h_ref):
    hg, n = SB_PRE_HEADS, HEAD_DIM
    t = x_ref.shape[1]

    @pl.when(pl.program_id(1) == 0)
    def _():
        h_ref[...] = _rms(x_ref[0], ng_ref[...]).astype(bf16)

    h = h_ref[...]
    q = jnp.dot(h, wq_ref[...], preferred_element_type=f32)
    ms = _head_sum(q * q, eseg_ref, eexp_ref) * (1.0 / n)
    q_out[0] = q * _head_expand(lax.rsqrt(ms + RMS_EPS), eexp_ref) * qg_ref[...]

    kv = lax.dot_general(wkv_ref[0], h, _NT, preferred_element_type=f32)
    k = kv[:hg * n].reshape(hg, n, t)
    ms = jnp.mean(k * k, axis=1, keepdims=True)
    kt_out[0] = k * lax.rsqrt(ms + RMS_EPS) * kg_ref[...]
    vt_out[0] = kv[hg * n:].reshape(hg, n, t)


def _sb_pre(x, ng, wq, wkv_t, qg, kg_col, eseg, eexp):
    b, t, d = x.shape
    hg, n = SB_PRE_HEADS, HEAD_DIM
    w = hg * n
    ins = [x, ng, wq, wkv_t, qg, kg_col, eseg[:w], eexp[:, :w]]
    tspec = pl.BlockSpec((1, hg, n, t), lambda bi, g: (bi, g, 0, 0))
    tout = jax.ShapeDtypeStruct((b, N_HEADS, n, t), f32)
    return pl.pallas_call(
        _sb_pre_kernel,
        grid=(b, N_HEADS // hg),
        in_specs=[
            pl.BlockSpec((1, t, d), lambda bi, g: (bi, 0, 0), pipeline_mode=pl.Buffered(1)),
            _resident(ng.shape),
            pl.BlockSpec((d, w), lambda bi, g: (0, g)),
            pl.BlockSpec((1, 2 * w, d), lambda bi, g: (g, 0, 0)),
            pl.BlockSpec((1, w), lambda bi, g: (0, g)),
            _resident(kg_col.shape),
            _resident((w, LANES)),
            _resident((LANES, w)),
        ],
        out_specs=[pl.BlockSpec((1, t, w), lambda bi, g: (bi, 0, g)), tspec, tspec],
        out_shape=[jax.ShapeDtypeStruct((b, t, d), f32), tout, tout],
        scratch_shapes=[pltpu.VMEM((t, d), bf16)],
        compiler_params=_params(("parallel", "arbitrary")),
        name="sb_pre",
    )(*ins)


SB_QBLOCK = 128
SB_WINDOW = 3 * SB_QBLOCK
SB_QBLOCKS_PER_STEP = 2
SB_PAST_WINDOW = 256
SB_SAMPLE_HEADS = 8


def _later_matrix(n):
    i = jnp.arange(n)
    return (i[:, None] > i[None, :]).astype(bf16)


def _sb_windows(qs, kts, vts, tri, carries, masks):
    n = range(len(qs))
    z = [_mm3(qs[i], kts[i]) * (HEAD_DIM ** -0.5) for i in n]
    log_beta = [jnp.minimum(x, 0.0) - jnp.log1p(jnp.exp(-jnp.abs(x))) for x in z]
    log_1m = [log_beta[i] - z[i] for i in n]
    log_1m = [x if m is None else jnp.where(m, x, 0.0) for x, m in zip(log_1m, masks)]
    later = [_mm_exact_rhs(x, tri) for x in log_1m]
    a = [jnp.exp(log_beta[i] + later[i] + carries[i]) for i in n]
    a = [x if m is None else jnp.where(m, x, 0.0) for x, m in zip(a, masks)]
    outs = [_mm(a[i], vts[i], _NT) for i in n]
    return outs, [carries[i] + later[i][:, :1] + log_1m[i][:, :1] for i in n]


def _any_alive(kends, carries):
    flags = [jnp.logical_and(k > 0, jnp.max(c) > DEAD_LOG) for k, c in zip(kends, carries)]
    return functools.reduce(jnp.logical_or, flags)


def _sb_prompt_kernel(q_ref, kt_ref, vt_ref, tri_ref, o_ref):
    step = pl.program_id(2)
    t = q_ref.shape[1]
    N, Q, W, G = HEAD_DIM, SB_QBLOCK, SB_WINDOW, SB_QBLOCKS_PER_STEP
    heads = range(LANES // N)
    tri = tri_ref[...]

    def window(ws, hd):
        return kt_ref[0, hd, :, pl.ds(ws, W)], vt_ref[0, hd, :, pl.ds(ws, W)]

    def earlier_windows(qs, chain_heads, kends, accs, carries):
        col = lax.broadcasted_iota(jnp.int32, (qs[0].shape[0], W), 1)

        def body(st):
            kends, accs, carries = st
            ws = [pl.multiple_of(jnp.maximum(k - W, 0), Q) for k in kends]
            kts, vts = zip(*[window(s, hd) for s, hd in zip(ws, chain_heads)])
            masks = [s + col < k for s, k in zip(ws, kends)]
            parts, carries = _sb_windows(qs, kts, vts, tri, carries, masks)
            return ws, [a + p for a, p in zip(accs, parts)], carries

        return lax.while_loop(lambda st: _any_alive(st[0], st[2]), body, (kends, accs, carries))[1]

    row = lax.broadcasted_iota(jnp.int32, (Q, W), 0)
    col = lax.broadcasted_iota(jnp.int32, (Q, W), 1)
    q0 = [pl.multiple_of((step * G + j) * Q, Q) for j in range(G)]
    ws = [pl.multiple_of(jnp.maximum(x - (W - Q), 0), Q) for x in q0]
    chains = [(j, hd) for j in range(G) for hd in heads]
    chain_heads = [hd for _, hd in chains]
    qs = [q_ref[0, pl.ds(q0[j], Q), hd * N:(hd + 1) * N] for j, hd in chains]
    kts, vts = zip(*[window(ws[j], hd) for j, hd in chains])
    masks = [ws[j] + col < q0[j] + row for j, _ in chains]
    zero = jnp.zeros((Q, 1), f32)
    accs, carries = _sb_windows(qs, kts, vts, tri, [zero] * len(chains), masks)
    accs = earlier_windows(qs, chain_heads, [ws[j] for j, _ in chains], accs, carries)
    for j in range(G):
        o_ref[0, pl.ds(q0[j], Q), :] = jnp.concatenate([accs[j * len(heads) + hd] for hd in heads], axis=1)

    q_full = (t // Q) * Q
    rem = t - q_full
    if rem:
        @pl.when(step == pl.num_programs(2) - 1)
        def _():
            qs = [q_ref[0, q_full:, hd * N:(hd + 1) * N] for hd in heads]
            kts = [kt_ref[0, hd, :, q_full:] for hd in heads]
            vts = [vt_ref[0, hd, :, q_full:] for hd in heads]
            causal = (lax.broadcasted_iota(jnp.int32, (rem, rem), 1)
                      < lax.broadcasted_iota(jnp.int32, (rem, rem), 0))
            zero = jnp.zeros((rem, 1), f32)
            accs, carries = _sb_windows(qs, kts, vts, tri[:rem, :rem], [zero] * len(heads), [causal] * len(heads))
            accs = earlier_windows(qs, list(heads), [jnp.int32(q_full)] * len(heads), accs, carries)
            o_ref[0, q_full:, :] = jnp.concatenate(accs, axis=1)


def _sb_prompt(q, kt, vt):
    b, t, d = q.shape
    n_full = t // SB_QBLOCK
    assert n_full % SB_QBLOCKS_PER_STEP == 0 and t >= SB_WINDOW
    tri = _later_matrix(SB_WINDOW)
    qspec = pl.BlockSpec((1, t, LANES), lambda bi, hp, s: (bi, 0, hp))
    kvspec = pl.BlockSpec((1, LANES // HEAD_DIM, HEAD_DIM, t), lambda bi, hp, s: (bi, hp, 0, 0))
    return pl.pallas_call(
        _sb_prompt_kernel,
        grid=(b, d // LANES, n_full // SB_QBLOCKS_PER_STEP),
        in_specs=[qspec, kvspec, kvspec, _resident(tri.shape)],
        out_specs=qspec,
        out_shape=jax.ShapeDtypeStruct((b, t, d), f32),
        compiler_params=_params(("parallel", "parallel", "arbitrary")),
        name="sb_prompt",
    )(q, kt, vt, tri)


def _sb_sample_kernel(q_ref, ktn_ref, vtn_ref, ktp_ref, vtp_ref, kt_hbm, vt_hbm, tri_ref, o_ref, kbuf, vbuf, sem):
    bi, g = pl.program_id(0), pl.program_id(1)
    N = HEAD_DIM
    hg, t, wp = ktn_ref.shape[1], q_ref.shape[1], ktp_ref.shape[3]
    heads = range(hg)
    tri = tri_ref[...]

    qs = [q_ref[0, :, hd * N:(hd + 1) * N] for hd in heads]
    kts = [jnp.concatenate([ktp_ref[0, hd], ktn_ref[0, hd]], axis=1) for hd in heads]
    vts = [jnp.concatenate([vtp_ref[0, hd], vtn_ref[0, hd]], axis=1) for hd in heads]
    row = lax.broadcasted_iota(jnp.int32, (t, wp + t), 0)
    col = lax.broadcasted_iota(jnp.int32, (t, wp + t), 1)
    mask = col - wp < row
    accs, carries = _sb_windows(qs, kts, vts, tri, [jnp.zeros((t, 1), f32)] * hg, [mask] * hg)

    def fetch(src, dst, ws, slot):
        return pltpu.make_async_copy(src.at[bi, pl.ds(g * hg, hg), :, pl.ds(ws, wp)], dst, sem.at[slot])

    def body(st):
        kend, accs, carries = st
        ws = pl.multiple_of(kend - wp, wp)
        copies = [fetch(kt_hbm, kbuf, ws, 0), fetch(vt_hbm, vbuf, ws, 1)]
        for c in copies:
            c.start()
        for c in copies:
            c.wait()
        parts, carries = _sb_windows(qs, [kbuf[hd] for hd in heads], [vbuf[hd] for hd in heads],
                                     tri[:wp, :wp], carries, [None] * hg)
        return ws, [a + p for a, p in zip(accs, parts)], carries

    kend = jnp.int32(kt_hbm.shape[3] - wp)
    accs = lax.while_loop(lambda st: _any_alive([st[0]] * hg, st[2]), body, (kend, accs, carries))[1]
    o_ref[0] = jnp.concatenate(accs, axis=1)


def _sb_sample(q, kt, vt, kt_past, vt_past):
    b, t, d = q.shape
    hg, n, wp = SB_SAMPLE_HEADS, HEAD_DIM, SB_PAST_WINDOW
    p = kt_past.shape[3]
    assert p % wp == 0
    tri = _later_matrix(wp + t)
    qspec = pl.BlockSpec((1, t, hg * n), lambda bi, g: (bi, 0, g))
    new = pl.BlockSpec((1, hg, n, t), lambda bi, g: (bi, g, 0, 0))
    last = pl.BlockSpec((1, hg, n, wp), lambda bi, g: (bi, g, 0, p // wp - 1))
    hbm = pl.BlockSpec(memory_space=pl.ANY)
    return pl.pallas_call(
        _sb_sample_kernel,
        grid=(b, N_HEADS // hg),
        in_specs=[qspec, new, new, last, last, hbm, hbm, _resident(tri.shape)],
        out_specs=qspec,
        out_shape=jax.ShapeDtypeStruct((b, t, d), f32),
        scratch_shapes=[pltpu.VMEM((hg, n, wp), f32), pltpu.VMEM((hg, n, wp), f32), pltpu.SemaphoreType.DMA((2,))],
        compiler_params=_params(("parallel", "parallel")),
        name="sb_sample",
    )(q, kt, vt, kt_past, vt_past, kt_past, vt_past, tri)


def _grouped_kv_t(w_kv):
    d = w_kv.shape[0]
    gw = SB_PRE_HEADS * HEAD_DIM
    k_t = w_kv[:, :d].T.reshape(d // gw, gw, d)
    v_t = w_kv[:, d:].T.reshape(d // gw, gw, d)
    return jnp.concatenate([k_t, v_t], axis=1)


def _stack(x, s0, shift0, kt_past, vt_past, w, tm, tt):
    b, t, d = x.shape
    m = b * t
    ffn = lambda x2, i, j: _ffn(x2, w["ffn_g"][i][j], w["ffn_wg"][i][j], w["ffn_wu"][i][j], w["ffn_wo"][i][j], tm)
    x2 = x.reshape(m, d)

    x2 = ffn(x2, 0, 0)
    rw = w["rwkv"]
    r, lw, k, v, kk, bb, g, shift = _rwkv_pre(x2.reshape(b, t, d), shift0.reshape(b, 1, d), w["mix_g"][0], rw,
                                              w["eseg"], w["eexp"], tt)
    y, s_fin = _rwkv_scan(r, lw, k, v, kk, bb, s0, rw, w["tri_chunk"])
    x2 = _proj_out(y.reshape(m, d), g.reshape(m, d), x2, rw["w_o"], tm)
    x2 = ffn(x2, 0, 1)

    x2 = ffn(x2, 1, 0)
    sb = w["sb"]
    q, kt, vt = _sb_pre(x2.reshape(b, t, d), w["mix_g"][1], sb["w_q"], sb["w_kv_t"], sb["q_g"], sb["k_g_col"],
                        w["eseg"], w["eexp"])
    if kt_past is None:
        o = _sb_prompt(q, kt, vt)
    else:
        o = _sb_sample(q, kt, vt, kt_past, vt_past)
    x2 = _proj_out(o.reshape(m, d), None, x2, sb["w_o"], tm)
    x2 = ffn(x2, 1, 1)
    return x2.reshape(b, t, d), s_fin, shift.reshape(b, d), kt, vt


def kernel(x_prompt, x_sample, state_rwkv_wkv, state_rwkv_shift, cache_sb_k, cache_sb_v, meta_tokens,
           ffn_norm_g, ffn_w_in, ffn_w_out, mix_norm_g, rwkv_mu, rwkv_w_rkv, rwkv_w0, rwkv_w1, rwkv_w2,
           rwkv_a0, rwkv_a1, rwkv_a2, rwkv_g1, rwkv_g2, rwkv_k_k, rwkv_k_a, rwkv_r_k, rwkv_gn_w, rwkv_gn_b,
           rwkv_w_o, sb_w_qkv, sb_q_norm_g, sb_k_norm_g, sb_w_o):
    d = D_MODEL
    depth = ffn_w_in.shape[0]
    d_ff = ffn_w_out.shape[2]
    pad = (-d_ff) % FF_CHUNK
    w_gate = jnp.pad(ffn_w_in[..., :d_ff], ((0, 0), (0, 0), (0, 0), (0, pad))).astype(bf16)
    w_up = jnp.pad(ffn_w_in[..., d_ff:], ((0, 0), (0, 0), (0, 0), (0, pad))).astype(bf16)
    w_out = jnp.pad(ffn_w_out, ((0, 0), (0, 0), (0, pad), (0, 0))).astype(bf16)
    vec = lambda a: a.reshape(1, d).astype(f32)
    head_of_lane = jnp.arange(d) // HEAD_DIM
    eseg = (head_of_lane[:, None] == jnp.arange(LANES)[None, :]).astype(bf16)
    w = {
        "ffn_g": [[vec(ffn_norm_g[i, j]) for j in range(2)] for i in range(depth)],
        "ffn_wg": w_gate, "ffn_wu": w_up, "ffn_wo": w_out,
        "mix_g": [vec(mix_norm_g[i]) for i in range(depth)],
        "eseg": eseg, "eexp": eseg.T,
        "tri_chunk": (jnp.arange(CHUNK)[:, None] >= jnp.arange(CHUNK)[None, :]).astype(bf16),
        "rwkv": {
            "mu": rwkv_mu, "w_rkv": rwkv_w_rkv.astype(bf16), "w0": vec(rwkv_w0), "w1": rwkv_w1.astype(bf16),
            "w2": rwkv_w2.astype(bf16), "a0": vec(rwkv_a0), "a1": rwkv_a1.astype(bf16), "a2": rwkv_a2.astype(bf16),
            "g1": rwkv_g1.astype(bf16), "g2": rwkv_g2.astype(bf16), "k_k": vec(rwkv_k_k), "k_a": vec(rwkv_k_a),
            "r_k": vec(rwkv_r_k), "gn_w": vec(rwkv_gn_w), "gn_b": vec(rwkv_gn_b), "w_o": rwkv_w_o.astype(bf16),
        },
        "sb": {
            "w_q": sb_w_qkv[:, :d].astype(bf16), "w_kv_t": _grouped_kv_t(sb_w_qkv[:, d:]).astype(bf16),
            "q_g": vec(jnp.tile(sb_q_norm_g, N_HEADS)), "k_g_col": sb_k_norm_g.reshape(HEAD_DIM, 1).astype(f32),
            "w_o": sb_w_o.astype(bf16),
        },
    }

    bp = x_prompt.shape[0]
    meta = jnp.broadcast_to(meta_tokens.astype(x_prompt.dtype)[None], (bp, N_META, d))
    xp = jnp.concatenate([meta, x_prompt], axis=1)
    s0_p = jnp.zeros((bp, N_HEADS, HEAD_DIM, HEAD_DIM), f32)
    shift0_p = jnp.zeros((bp, d), f32)
    yp, s_p, shift_p, kt_p, vt_p = _stack(xp, s0_p, shift0_p, None, None, w, tm=688, tt=344)
    frames_last = lambda a: jnp.swapaxes(a, 2, 3)
    ys, s_s, shift_s, kt_s, vt_s = _stack(x_sample, state_rwkv_wkv, state_rwkv_shift, frames_last(cache_sb_k),
                                          frames_last(cache_sb_v), w, tm=512, tt=64)
    return (yp[:, N_META:], ys, s_p, shift_p, frames_last(kt_p), frames_last(vt_p), s_s, shift_s,
            frames_last(kt_s), frames_last(vt_s))
```

```python
import functools

import jax
import jax.numpy as jnp
from jax import lax
from jax.experimental import pallas as pl
from jax.experimental.pallas import tpu as pltpu

D_MODEL = 1024
HEAD_DIM = 64
N_HEADS = D_MODEL // HEAD_DIM
N_META = 16
RMS_EPS = 1e-6
GN_EPS = 64e-5

LANES = 128
CHUNK = 64
SCAN_CHUNKS_PER_STEP = 3
FF_CHUNK = 256
DEAD_LOG = -104.0
VMEM_LIMIT = 52 * 1024 * 1024

f32 = jnp.float32
bf16 = jnp.bfloat16

_NN = (((1,), (0,)), ((), ()))
_NT = (((1,), (1,)), ((), ()))
_TN = (((0,), (0,)), ((), ()))


def _mm(a, b, dims=_NN):
    return lax.dot_general(a.astype(bf16), b.astype(bf16), dims, preferred_element_type=f32)


def _split(a):
    hi = a.astype(bf16)
    lo = (a - hi.astype(f32)).astype(bf16)
    return hi, lo


def _mm3(a, b, dims=_NN):
    ah, al = _split(a)
    bh, bl = _split(b)
    dot = functools.partial(lax.dot_general, dimension_numbers=dims, preferred_element_type=f32)
    return dot(ah, bh) + (dot(ah, bl) + dot(al, bh))


def _mm_exact_rhs(a, b_bf16, dims=_NN):
    ah, al = _split(a)
    dot = functools.partial(lax.dot_general, dimension_numbers=dims, preferred_element_type=f32)
    return dot(ah, b_bf16) + dot(al, b_bf16)


def _mm_exact_lhs(a_bf16, b, dims=_NN):
    bh, bl = _split(b)
    dot = functools.partial(lax.dot_general, dimension_numbers=dims, preferred_element_type=f32)
    return dot(a_bf16, bh) + dot(a_bf16, bl)


def _rms(x, g):
    return x * lax.rsqrt(jnp.mean(x * x, axis=-1, keepdims=True) + RMS_EPS) * g


def _softplus(y):
    return jnp.maximum(y, 0.0) + jnp.log1p(jnp.exp(-jnp.abs(y)))


def _sigmoid(y):
    return 1.0 / (1.0 + jnp.exp(-y))


def _head_sum(x, eseg_ref, eexp_ref):
    return _mm_exact_rhs(x, eseg_ref[...])


def _head_expand(y, eexp_ref):
    return _mm_exact_rhs(y, eexp_ref[...])


def _full(shape):
    return pl.BlockSpec(shape, lambda *_: (0,) * len(shape))


def _resident(shape):
    return pl.BlockSpec(shape, lambda *_: (0,) * len(shape), pipeline_mode=pl.Buffered(1))


def _params(sem):
    return pltpu.CompilerParams(dimension_semantics=sem, vmem_limit_bytes=VMEM_LIMIT)


def _ffn_kernel(n_mix, *refs):
    mix, (x_ref, g_ref, wg_ref, wu_ref, wo_ref, o_ref, acc_ref) = refs[:n_mix], refs[n_mix:]
    rows = lambda ref: ref[...].reshape(ref.shape[-2:])
    x = rows(x_ref)
    if n_mix:
        y = rows(mix[0])
        if n_mix == 3:
            y = y * rows(mix[1])
        x = x + jnp.dot(y.astype(bf16), mix[-1][...], preferred_element_type=f32)
    h = _rms(x, g_ref[...]).astype(bf16)
    n_chunks = wg_ref.shape[1] // FF_CHUNK
    for c in range(n_chunks):
        sl = slice(c * FF_CHUNK, (c + 1) * FF_CHUNK)
        gate = jnp.dot(h, wg_ref[:, sl], preferred_element_type=f32)
        up = jnp.dot(h, wu_ref[:, sl], preferred_element_type=f32)
        act = (gate * _sigmoid(gate) * up).astype(bf16)
        part = jnp.dot(act, wo_ref[sl, :], preferred_element_type=f32)
        if c == 0:
            acc_ref[...] = part
        else:
            acc_ref[...] += part
    o_ref[...] = x + 0.5 * acc_ref[...]


def _ffn(x, g, wg, wu, wo, tm, mix=None, skip=0):
    b, t, d = x.shape
    fp = wg.shape[1]
    assert (t - skip) % tm == 0
    if skip:
        assert skip % 8 == 0 and tm % 8 == 0
        row_in = pl.BlockSpec((pl.Element(1), pl.Element(tm), pl.Element(d)),
                              lambda bi, i: (bi, pl.multiple_of(skip + i * tm, 8), 0))
    else:
        row_in = pl.BlockSpec((None, tm, d), lambda bi, i: (bi, i, 0))
    row_out = pl.BlockSpec((None, tm, d), lambda bi, i: (bi, i, 0))
    mix_ins, mix_specs = [], []
    if mix is not None:
        y, gate, w_proj = mix
        mix_ins = [y] + ([gate] if gate is not None else []) + [w_proj]
        mix_specs = [row_in] * (len(mix_ins) - 1) + [_resident(w_proj.shape)]
    return pl.pallas_call(
        functools.partial(_ffn_kernel, len(mix_ins)),
        grid=(b, (t - skip) // tm),
        in_specs=mix_specs + [row_in, _full((1, d)), _resident((d, fp)), _resident((d, fp)), _resident((fp, d))],
        out_specs=row_out,
        out_shape=jax.ShapeDtypeStruct((b, t - skip, d), f32),
        scratch_shapes=[pltpu.VMEM((tm, d), f32)],
        compiler_params=_params(("parallel", "parallel")),
        name="ffn",
    )(*mix_ins, x, g, wg, wu, wo)


def _rwkv_pre_kernel(x_ref, shift_ref, ng_ref, mu_ref, wrkv_ref, w0_ref, w1_ref, w2_ref,
                     a0_ref, a1_ref, a2_ref, g1_ref, g2_ref, kk_ref, ka_ref, eseg_ref, eexp_ref,
                     r_out, lw_out, k_out, v_out, kk_out, b_out, g_out, shift_out, prev_ref):
    i = pl.program_id(1)
    tt = x_ref.shape[1]

    @pl.when(i == 0)
    def _():
        prev_ref[...] = shift_ref[0]

    h = _rms(x_ref[0], ng_ref[...])
    first_row = lax.broadcasted_iota(jnp.int32, h.shape, 0) == 0
    h_prev = jnp.where(first_row, prev_ref[...], pltpu.roll(h, 1, axis=0))
    last = h[tt - 1:tt, :]
    prev_ref[...] = last
    shift_out[0] = last

    dx = h_prev - h
    xr, xw, xk, xv, xa, xg = ((h + dx * mu_ref[j:j + 1, :]).astype(bf16) for j in range(6))
    r = jnp.dot(xr, wrkv_ref[0], preferred_element_type=f32)
    k = jnp.dot(xk, wrkv_ref[1], preferred_element_type=f32)
    v = jnp.dot(xv, wrkv_ref[2], preferred_element_type=f32)
    lora_w = _mm(jnp.tanh(jnp.dot(xw, w1_ref[...], preferred_element_type=f32)), w2_ref[...])
    log_rate = -_softplus(-(w0_ref[...] + lora_w)) - 0.5
    a = _sigmoid(a0_ref[...] + _mm(jnp.dot(xa, a1_ref[...], preferred_element_type=f32), a2_ref[...]))
    g = _mm(_sigmoid(jnp.dot(xg, g1_ref[...], preferred_element_type=f32)), g2_ref[...])

    kk = k * kk_ref[...]
    norm = jnp.sqrt(_head_sum(kk * kk, eseg_ref, eexp_ref))
    kk = kk * _head_expand(1.0 / jnp.maximum(norm, 1e-12), eexp_ref)

    r_out[0] = r
    lw_out[0] = -jnp.exp(log_rate)
    k_out[0] = k * (1.0 + (a - 1.0) * ka_ref[...])
    v_out[0] = v
    kk_out[0] = kk
    b_out[0] = kk * a
    g_out[0] = g


def _rwkv_pre(x, shift0, ng, p, eseg, eexp, tt):
    b, t, d = x.shape
    seq = pl.BlockSpec((1, tt, d), lambda bi, i: (bi, i, 0))
    vec = pl.BlockSpec((1, 1, d), lambda bi, i: (bi, 0, 0))
    out = jax.ShapeDtypeStruct((b, t, d), f32)
    ins = [x, shift0, ng, p["mu"], p["w_rkv"], p["w0"], p["w1"], p["w2"], p["a0"], p["a1"], p["a2"],
           p["g1"], p["g2"], p["k_k"], p["k_a"], eseg, eexp]
    specs = [seq, vec] + [_resident(a.shape) for a in ins[2:]]
    return pl.pallas_call(
        _rwkv_pre_kernel,
        grid=(b, t // tt),
        in_specs=specs,
        out_specs=[seq] * 7 + [vec],
        out_shape=[out] * 7 + [jax.ShapeDtypeStruct((b, 1, d), f32)],
        scratch_shapes=[pltpu.VMEM((1, d), f32)],
        compiler_params=_params(("parallel", "arbitrary")),
        name="rwkv_pre",
    )(*ins)


def _rwkv_scan_kernel(t_total, n_sub, r_ref, lw_ref, k_ref, v_ref, kk_ref, b_ref, s0_ref, rk_ref, gnw_ref, gnb_ref,
                      tri_ref, y_ref, s_ref, sbd_ref):
    step = pl.program_id(1)
    C, N = CHUNK, HEAD_DIM
    per_tile = LANES // N
    pairs = range(N_HEADS // per_tile)
    tiles = [slice(p * LANES, (p + 1) * LANES) for p in pairs]

    @pl.when(step == 0)
    def _():
        zero = jnp.zeros((N, N), f32)
        for p in pairs:
            sbd_ref[p] = jnp.concatenate(
                [jnp.concatenate([s0_ref[0, per_tile * p], zero], axis=1),
                 jnp.concatenate([zero, s0_ref[0, per_tile * p + 1]], axis=1)], axis=0)

    row = lax.broadcasted_iota(jnp.int32, (C, 2 * C), 0)
    lane = lax.broadcasted_iota(jnp.int32, (C, 2 * C), 1)
    left = lane < C
    col = jnp.where(left, lane, lane - C)
    strict = row > col
    incl = row >= col
    eye_right = jnp.logical_and(row == col, jnp.logical_not(left)).astype(f32)
    head_lanes = [(lax.broadcasted_iota(jnp.int32, (2 * C, LANES), 1) // N) == s for s in range(per_tile)]
    first = lax.broadcasted_iota(jnp.int32, (C, LANES), 1) < N

    chunks = []
    for j in range(n_sub):
        rows = slice(j * C, (j + 1) * C)
        r, lw, k, v, kk, b = (ref[0, rows] for ref in (r_ref, lw_ref, k_ref, v_ref, kk_ref, b_ref))
        if t_total % (n_sub * C):
            valid = (step * n_sub + j) * C + lax.broadcasted_iota(jnp.int32, r.shape, 0) < t_total
            r, lw, k, v, kk, b = (jnp.where(valid, a, 0.0) for a in (r, lw, k, v, kk, b))
        cum = _mm_exact_lhs(tri_ref[...], lw)
        p_t = jnp.exp(cum)
        p_inv = jnp.exp(-cum)
        r_t = r * p_t
        a_t = -kk * jnp.exp(cum - lw)
        b_t = b * p_inv
        k_t = k * p_inv
        gamma = p_t[C - 1:C, :]
        b_g = b_t * gamma
        k_g = k_t * gamma
        chunks.append(dict(
            rows=rows, gamma=gamma, bonus=r * k * rk_ref[...],
            ar=[jnp.concatenate([a_t[:, sl], r_t[:, sl]], axis=0) for sl in tiles],
            bbbk=[jnp.concatenate([b_t[:, sl]] * 3 + [k_t[:, sl]], axis=0) for sl in tiles],
            bkg=[jnp.concatenate([b_g[:, sl], k_g[:, sl]], axis=0) for sl in tiles],
            v=[v[:, sl] for sl in tiles]))

    chains = [(j, p, s) for j in range(n_sub) for p in pairs for s in range(per_tile)]
    ar = [jnp.where(head_lanes[s], chunks[j]["ar"][p], 0.0) for j, p, s in chains]
    x4 = [_mm(ar[i], chunks[j]["bbbk"][p], _NT) for i, (j, p, _) in enumerate(chains)]
    z = [jnp.where(jnp.logical_and(strict, left), x[:C, :2 * C], 0.0) + eye_right for x in x4]
    for _ in range(6):
        sq = [_mm(x[:, :C], x) for x in z]
        z = [jnp.where(left, q, x + q) for x, q in zip(z, sq)]
    t_inv = [jnp.where(left, 0.0, x) for x in z]
    a_ak = [jnp.where(jnp.logical_and(strict, jnp.logical_not(left)), x[:C, 2 * C:], 0.0) for x in x4]
    akv = [_mm(a_ak[i], jnp.concatenate([chunks[j]["v"][p]] * 2, axis=0)) for i, (j, p, _) in enumerate(chains)]
    a_r = [jnp.where(incl, x[C:, 2 * C:], 0.0) for x in x4]

    def head_sum(x):
        s0 = jnp.sum(jnp.where(first, x, 0.0), axis=-1, keepdims=True)
        s1 = jnp.sum(jnp.where(first, 0.0, x), axis=-1, keepdims=True)
        return jnp.where(first, s0, s1)

    for j in range(n_sub):
        ck = chunks[j]
        mine = [i for i, (jj, _, _) in enumerate(chains) if jj == j]
        s_old = [sbd_ref[p] for p in pairs]
        xs = {i: _mm_exact_lhs(ar[i].astype(bf16), s_old[chains[i][1]], _NT) for i in mine}
        rhs = {i: xs[i][:C] + akv[i] for i in mine}
        u = {i: _mm(t_inv[i], jnp.concatenate([rhs[i], rhs[i]], axis=0)) for i in mine}
        uv = {i: jnp.concatenate([u[i], ck["v"][chains[i][1]]], axis=0) for i in mine}
        o = {i: xs[i][C:] + _mm(a_r[i], uv[i]) for i in mine}
        s_add = {i: _mm(uv[i], jnp.where(head_lanes[chains[i][2]], ck["bkg"][chains[i][1]], 0.0), _TN) for i in mine}
        y_tiles = []
        for p in pairs:
            i0, sl = mine[per_tile * p], tiles[p]
            sbd_ref[p] = (s_old[p] * ck["gamma"][:, sl]
                          + jnp.concatenate([s_add[i0][:N], s_add[i0 + 1][N:]], axis=0))
            o_pair = jnp.where(first, o[i0], o[i0 + 1])
            dev = o_pair - head_sum(o_pair) * (1.0 / N)
            o_n = dev * lax.rsqrt(head_sum(dev * dev) * (1.0 / N) + GN_EPS) * gnw_ref[:, sl] + gnb_ref[:, sl]
            y_tiles.append(o_n + head_sum(ck["bonus"][:, sl]) * ck["v"][p])
        y_ref[0, ck["rows"]] = jnp.concatenate(y_tiles, axis=1)

    @pl.when(step == pl.num_programs(1) - 1)
    def _():
        for p in pairs:
            s_ref[0, per_tile * p] = sbd_ref[p][:N, :N]
            s_ref[0, per_tile * p + 1] = sbd_ref[p][N:, N:]


def _rwkv_scan(r, lw, k, v, kk, b, s0, p, tri):
    bsz, t, d = r.shape
    n_sub = min(SCAN_CHUNKS_PER_STEP, pl.cdiv(t, CHUNK))
    seq = pl.BlockSpec((1, n_sub * CHUNK, d), lambda bi, c: (bi, c, 0))
    st = pl.BlockSpec((1, N_HEADS, HEAD_DIM, HEAD_DIM), lambda bi, c: (bi, 0, 0, 0))
    consts = [p["r_k"], p["gn_w"], p["gn_b"], tri]
    return pl.pallas_call(
        functools.partial(_rwkv_scan_kernel, t, n_sub),
        grid=(bsz, pl.cdiv(t, n_sub * CHUNK)),
        in_specs=[seq] * 6 + [st] + [_resident(a.shape) for a in consts],
        out_specs=[seq, st],
        out_shape=[jax.ShapeDtypeStruct((bsz, t, d), f32), jax.ShapeDtypeStruct(s0.shape, f32)],
        scratch_shapes=[pltpu.VMEM((N_HEADS * HEAD_DIM // LANES, LANES, LANES), f32)],
        compiler_params=_params(("parallel", "arbitrary")),
        name="rwkv_scan",
    )(r, lw, k, v, kk, b, s0, *consts)


SB_PRE_HEADS = 4


def _sb_pre_kernel(reuse_h, x_ref, ng_ref, wq_ref, wkv_ref, qg_ref, kg_ref, eseg_ref, eexp_ref, q_out, kt_out, vt_out,
                   h_ref):
    hg, n = SB_PRE_HEADS, HEAD_DIM
    t = x_ref.shape[1]

    def normed():
        return _rms(x_ref[0], ng_ref[...]).astype(bf16)

    if reuse_h:
        @pl.when(pl.program_id(1) == 0)
        def _():
            h_ref[...] = normed()
        h = h_ref[...]
    else:
        h = normed()
    q = jnp.dot(h, wq_ref[...], preferred_element_type=f32)
    ms = _head_sum(q * q, eseg_ref, eexp_ref) * (1.0 / n)
    q_out[0] = q * _head_expand(lax.rsqrt(ms + RMS_EPS), eexp_ref) * qg_ref[...]

    kv = lax.dot_general(wkv_ref[0], h, _NT, preferred_element_type=f32)
    k = kv[:hg * n].reshape(hg, n, t)
    ms = jnp.mean(k * k, axis=1, keepdims=True)
    kt_out[0] = k * lax.rsqrt(ms + RMS_EPS) * kg_ref[...]
    vt_out[0] = kv[hg * n:].reshape(hg, n, t)


def _sb_pre(x, ng, wq, wkv_t, qg, kg_col, eseg, eexp, streams_inner):
    b, t, d = x.shape
    hg, n = SB_PRE_HEADS, HEAD_DIM
    w = hg * n
    ins = [x, ng, wq, wkv_t, qg, kg_col, eseg[:w], eexp[:, :w]]
    if streams_inner:
        grid = (N_HEADS // hg, b)
        ix = lambda f: (lambda g, bi: f(bi, g))
    else:
        grid = (b, N_HEADS // hg)
        ix = lambda f: f
    tspec = pl.BlockSpec((1, hg, n, t), ix(lambda bi, g: (bi, g, 0, 0)))
    tout = jax.ShapeDtypeStruct((b, N_HEADS, n, t), f32)
    return pl.pallas_call(
        functools.partial(_sb_pre_kernel, not streams_inner),
        grid=grid,
        in_specs=[
            pl.BlockSpec((1, t, d), ix(lambda bi, g: (bi, 0, 0)),
                         pipeline_mode=None if streams_inner else pl.Buffered(1)),
            _resident(ng.shape),
            pl.BlockSpec((d, w), ix(lambda bi, g: (0, g))),
            pl.BlockSpec((1, 2 * w, d), ix(lambda bi, g: (g, 0, 0))),
            pl.BlockSpec((1, w), ix(lambda bi, g: (0, g))),
            _resident(kg_col.shape),
            _resident((w, LANES)),
            _resident((LANES, w)),
        ],
        out_specs=[pl.BlockSpec((1, t, w), ix(lambda bi, g: (bi, 0, g))), tspec, tspec],
        out_shape=[jax.ShapeDtypeStruct((b, t, d), f32), tout, tout],
        scratch_shapes=[pltpu.VMEM((t, d), bf16)],
        compiler_params=_params(("arbitrary", "arbitrary")),
        name="sb_pre",
    )(*ins)


SB_QBLOCK = 128
SB_WINDOW = 3 * SB_QBLOCK
SB_QBLOCKS_PER_STEP = 4
SB_PAST_WINDOW = 256
SB_SAMPLE_HEADS = 8


def _later_matrix(n):
    i = jnp.arange(n)
    return (i[:, None] > i[None, :]).astype(bf16)


def _sb_windows(qs, kts, vts, tri, carries, masks):
    n = range(len(qs))
    z = [_mm(qs[i], kts[i]) * (HEAD_DIM ** -0.5) for i in n]
    log_beta = [jnp.minimum(x, 0.0) - jnp.log(1.0 + jnp.exp(-jnp.abs(x))) for x in z]
    log_1m = [log_beta[i] - z[i] for i in n]
    log_1m = [x if m is None else jnp.where(m, x, 0.0) for x, m in zip(log_1m, masks)]
    later = [_mm_exact_rhs(x, tri) for x in log_1m]
    a = [jnp.exp(log_beta[i] + later[i] + carries[i]) for i in n]
    a = [x if m is None else jnp.where(m, x, 0.0) for x, m in zip(a, masks)]
    outs = [_mm(a[i], vts[i], _NT) for i in n]
    return outs, [carries[i] + later[i][:, :1] + log_1m[i][:, :1] for i in n]


def _any_alive(kends, carries):
    flags = [jnp.logical_and(k > 0, jnp.max(c) > DEAD_LOG) for k, c in zip(kends, carries)]
    return functools.reduce(jnp.logical_or, flags)


def _sb_prompt_kernel(q_ref, kt_ref, vt_ref, tri_ref, o_ref):
    step = pl.program_id(2)
    t = q_ref.shape[1]
    N, Q, W, G = HEAD_DIM, SB_QBLOCK, SB_WINDOW, SB_QBLOCKS_PER_STEP
    heads = range(LANES // N)
    tri = tri_ref[...]

    def window(ws, hd):
        return kt_ref[0, hd, :, pl.ds(ws, W)], vt_ref[0, hd, :, pl.ds(ws, W)]

    def earlier_windows(qs, chain_heads, kends, accs, carries):
        col = lax.broadcasted_iota(jnp.int32, (qs[0].shape[0], W), 1)

        def body(st):
            kends, accs, carries = st
            ws = [pl.multiple_of(jnp.maximum(k - W, 0), Q) for k in kends]
            kts, vts = zip(*[window(s, hd) for s, hd in zip(ws, chain_heads)])
            masks = [s + col < k for s, k in zip(ws, kends)]
            parts, carries = _sb_windows(qs, kts, vts, tri, carries, masks)
            return ws, [a + p for a, p in zip(accs, parts)], carries

        return lax.while_loop(lambda st: _any_alive(st[0], st[2]), body, (kends, accs, carries))[1]

    row = lax.broadcasted_iota(jnp.int32, (Q, W), 0)
    col = lax.broadcasted_iota(jnp.int32, (Q, W), 1)
    q0 = [pl.multiple_of((step * G + j) * Q, Q) for j in range(G)]
    ws = [pl.multiple_of(jnp.maximum(x - (W - Q), 0), Q) for x in q0]
    chains = [(j, hd) for j in range(G) for hd in heads]
    chain_heads = [hd for _, hd in chains]
    qs = [q_ref[0, pl.ds(q0[j], Q), hd * N:(hd + 1) * N] for j, hd in chains]
    kts, vts = zip(*[window(ws[j], hd) for j, hd in chains])
    masks = [ws[j] + col < q0[j] + row for j, _ in chains]
    zero = jnp.zeros((Q, 1), f32)
    accs, carries = _sb_windows(qs, kts, vts, tri, [zero] * len(chains), masks)
    accs = earlier_windows(qs, chain_heads, [ws[j] for j, _ in chains], accs, carries)
    for j in range(G):
        o_ref[0, pl.ds(q0[j], Q), :] = jnp.concatenate([accs[j * len(heads) + hd] for hd in heads], axis=1)

    q_full = (t // Q) * Q
    rem = t - q_full
    if rem:
        @pl.when(step == pl.num_programs(2) - 1)
        def _():
            qs = [q_ref[0, q_full:, hd * N:(hd + 1) * N] for hd in heads]
            kts = [kt_ref[0, hd, :, q_full:] for hd in heads]
            vts = [vt_ref[0, hd, :, q_full:] for hd in heads]
            causal = (lax.broadcasted_iota(jnp.int32, (rem, rem), 1)
                      < lax.broadcasted_iota(jnp.int32, (rem, rem), 0))
            zero = jnp.zeros((rem, 1), f32)
            accs, carries = _sb_windows(qs, kts, vts, tri[:rem, :rem], [zero] * len(heads), [causal] * len(heads))
            accs = earlier_windows(qs, list(heads), [jnp.int32(q_full)] * len(heads), accs, carries)
            o_ref[0, q_full:, :] = jnp.concatenate(accs, axis=1)


def _sb_prompt(q, kt, vt):
    b, t, d = q.shape
    n_full = t // SB_QBLOCK
    assert n_full % SB_QBLOCKS_PER_STEP == 0 and t >= SB_WINDOW
    tri = _later_matrix(SB_WINDOW)
    qspec = pl.BlockSpec((1, t, LANES), lambda bi, hp, s: (bi, 0, hp))
    kvspec = pl.BlockSpec((1, LANES // HEAD_DIM, HEAD_DIM, t), lambda bi, hp, s: (bi, hp, 0, 0))
    return pl.pallas_call(
        _sb_prompt_kernel,
        grid=(b, d // LANES, n_full // SB_QBLOCKS_PER_STEP),
        in_specs=[qspec, kvspec, kvspec, _resident(tri.shape)],
        out_specs=qspec,
        out_shape=jax.ShapeDtypeStruct((b, t, d), f32),
        compiler_params=_params(("parallel", "parallel", "arbitrary")),
        name="sb_prompt",
    )(q, kt, vt, tri)


def _sb_sample_kernel(q_ref, ktn_ref, vtn_ref, ktp_ref, vtp_ref, kt_hbm, vt_hbm, tri_ref, o_ref, kbuf, vbuf, sem):
    bi, g = pl.program_id(0), pl.program_id(1)
    N = HEAD_DIM
    hg, t, wp = ktn_ref.shape[1], q_ref.shape[1], ktp_ref.shape[3]
    heads = range(hg)
    tri = tri_ref[...]

    qs = [q_ref[0, :, hd * N:(hd + 1) * N] for hd in heads]
    kts = [jnp.concatenate([ktp_ref[0, hd], ktn_ref[0, hd]], axis=1) for hd in heads]
    vts = [jnp.concatenate([vtp_ref[0, hd], vtn_ref[0, hd]], axis=1) for hd in heads]
    row = lax.broadcasted_iota(jnp.int32, (t, wp + t), 0)
    col = lax.broadcasted_iota(jnp.int32, (t, wp + t), 1)
    mask = col - wp < row
    accs, carries = _sb_windows(qs, kts, vts, tri, [jnp.zeros((t, 1), f32)] * hg, [mask] * hg)

    def fetch(src, dst, ws, slot):
        return pltpu.make_async_copy(src.at[bi, pl.ds(g * hg, hg), :, pl.ds(ws, wp)], dst, sem.at[slot])

    def body(st):
        kend, accs, carries = st
        ws = pl.multiple_of(kend - wp, wp)
        copies = [fetch(kt_hbm, kbuf, ws, 0), fetch(vt_hbm, vbuf, ws, 1)]
        for c in copies:
            c.start()
        for c in copies:
            c.wait()
        parts, carries = _sb_windows(qs, [kbuf[hd] for hd in heads], [vbuf[hd] for hd in heads],
                                     tri[:wp, :wp], carries, [None] * hg)
        return ws, [a + p for a, p in zip(accs, parts)], carries

    kend = jnp.int32(kt_hbm.shape[3] - wp)
    accs = lax.while_loop(lambda st: _any_alive([st[0]] * hg, st[2]), body, (kend, accs, carries))[1]
    o_ref[0] = jnp.concatenate(accs, axis=1)


def _sb_sample(q, kt, vt, kt_past, vt_past):
    b, t, d = q.shape
    hg, n, wp = SB_SAMPLE_HEADS, HEAD_DIM, SB_PAST_WINDOW
    p = kt_past.shape[3]
    assert p % wp == 0
    tri = _later_matrix(wp + t)
    qspec = pl.BlockSpec((1, t, hg * n), lambda bi, g: (bi, 0, g))
    new = pl.BlockSpec((1, hg, n, t), lambda bi, g: (bi, g, 0, 0))
    last = pl.BlockSpec((1, hg, n, wp), lambda bi, g: (bi, g, 0, p // wp - 1))
    hbm = pl.BlockSpec(memory_space=pl.ANY)
    return pl.pallas_call(
        _sb_sample_kernel,
        grid=(b, N_HEADS // hg),
        in_specs=[qspec, new, new, last, last, hbm, hbm, _resident(tri.shape)],
        out_specs=qspec,
        out_shape=jax.ShapeDtypeStruct((b, t, d), f32),
        scratch_shapes=[pltpu.VMEM((hg, n, wp), f32), pltpu.VMEM((hg, n, wp), f32), pltpu.SemaphoreType.DMA((2,))],
        compiler_params=_params(("parallel", "parallel")),
        name="sb_sample",
    )(q, kt, vt, kt_past, vt_past, kt_past, vt_past, tri)


def _grouped_kv_t(w_kv):
    d = w_kv.shape[0]
    gw = SB_PRE_HEADS * HEAD_DIM
    k_t = w_kv[:, :d].T.reshape(d // gw, gw, d)
    v_t = w_kv[:, d:].T.reshape(d // gw, gw, d)
    return jnp.concatenate([k_t, v_t], axis=1)


def _stack(x, s0, shift0, kt_past, vt_past, w, tm, tt, skip=0, tm_last=None):
    b, t, d = x.shape
    m = b * t
    flat = lambda a: a.reshape(1, m, d)

    def ffn(x2, i, j, mix=None):
        if mix is not None:
            mix = (flat(mix[0]), None if mix[1] is None else flat(mix[1]), mix[2])
        return _ffn(flat(x2), w["ffn_g"][i][j], w["ffn_wg"][i][j], w["ffn_wu"][i][j], w["ffn_wo"][i][j],
                    tm, mix).reshape(m, d)

    x2 = x.reshape(m, d)

    x2 = ffn(x2, 0, 0)
    rw = w["rwkv"]
    r, lw, k, v, kk, bb, g, shift = _rwkv_pre(x2.reshape(b, t, d), shift0.reshape(b, 1, d), w["mix_g"][0], rw,
                                              w["eseg"], w["eexp"], tt)
    y, s_fin = _rwkv_scan(r, lw, k, v, kk, bb, s0, rw, w["tri_chunk"])
    x2 = ffn(x2, 0, 1, (y, g, rw["w_o"]))

    x2 = ffn(x2, 1, 0)
    sb = w["sb"]
    q, kt, vt = _sb_pre(x2.reshape(b, t, d), w["mix_g"][1], sb["w_q"], sb["w_kv_t"], sb["q_g"], sb["k_g_col"],
                        w["eseg"], w["eexp"], streams_inner=4 * t < 6 * SB_PRE_HEADS * HEAD_DIM)
    if kt_past is None:
        o = _sb_prompt(q, kt, vt)
    else:
        o = _sb_sample(q, kt, vt, kt_past, vt_past)
    if skip:
        y = _ffn(x2.reshape(b, t, d), w["ffn_g"][1][1], w["ffn_wg"][1][1], w["ffn_wu"][1][1], w["ffn_wo"][1][1],
                 tm_last, (o, None, sb["w_o"]), skip)
    else:
        y = ffn(x2, 1, 1, (o, None, sb["w_o"])).reshape(b, t, d)
    return y, s_fin, shift.reshape(b, d), kt, vt


def kernel(x_prompt, x_sample, state_rwkv_wkv, state_rwkv_shift, cache_sb_k, cache_sb_v, meta_tokens,
           ffn_norm_g, ffn_w_in, ffn_w_out, mix_norm_g, rwkv_mu, rwkv_w_rkv, rwkv_w0, rwkv_w1, rwkv_w2,
           rwkv_a0, rwkv_a1, rwkv_a2, rwkv_g1, rwkv_g2, rwkv_k_k, rwkv_k_a, rwkv_r_k, rwkv_gn_w, rwkv_gn_b,
           rwkv_w_o, sb_w_qkv, sb_q_norm_g, sb_k_norm_g, sb_w_o):
    d = D_MODEL
    depth = ffn_w_in.shape[0]
    d_ff = ffn_w_out.shape[2]
    pad = (-d_ff) % FF_CHUNK
    w_gate = jnp.pad(ffn_w_in[..., :d_ff], ((0, 0), (0, 0), (0, 0), (0, pad))).astype(bf16)
    w_up = jnp.pad(ffn_w_in[..., d_ff:], ((0, 0), (0, 0), (0, 0), (0, pad))).astype(bf16)
    w_out = jnp.pad(ffn_w_out, ((0, 0), (0, 0), (0, pad), (0, 0))).astype(bf16)
    vec = lambda a: a.reshape(1, d).astype(f32)
    head_of_lane = jnp.arange(d) // HEAD_DIM
    eseg = (head_of_lane[:, None] == jnp.arange(LANES)[None, :]).astype(bf16)
    w = {
        "ffn_g": [[vec(ffn_norm_g[i, j]) for j in range(2)] for i in range(depth)],
        "ffn_wg": w_gate, "ffn_wu": w_up, "ffn_wo": w_out,
        "mix_g": [vec(mix_norm_g[i]) for i in range(depth)],
        "eseg": eseg, "eexp": eseg.T,
        "tri_chunk": (jnp.arange(CHUNK)[:, None] >= jnp.arange(CHUNK)[None, :]).astype(bf16),
        "rwkv": {
            "mu": rwkv_mu, "w_rkv": rwkv_w_rkv.astype(bf16), "w0": vec(rwkv_w0), "w1": rwkv_w1.astype(bf16),
            "w2": rwkv_w2.astype(bf16), "a0": vec(rwkv_a0), "a1": rwkv_a1.astype(bf16), "a2": rwkv_a2.astype(bf16),
            "g1": rwkv_g1.astype(bf16), "g2": rwkv_g2.astype(bf16), "k_k": vec(rwkv_k_k), "k_a": vec(rwkv_k_a),
            "r_k": vec(rwkv_r_k), "gn_w": vec(rwkv_gn_w), "gn_b": vec(rwkv_gn_b), "w_o": rwkv_w_o.astype(bf16),
        },
        "sb": {
            "w_q": sb_w_qkv[:, :d].astype(bf16), "w_kv_t": _grouped_kv_t(sb_w_qkv[:, d:]).astype(bf16),
            "q_g": vec(jnp.tile(sb_q_norm_g, N_HEADS)), "k_g_col": sb_k_norm_g.reshape(HEAD_DIM, 1).astype(f32),
            "w_o": sb_w_o.astype(bf16),
        },
    }

    bp = x_prompt.shape[0]
    meta = jnp.broadcast_to(meta_tokens.astype(x_prompt.dtype)[None], (bp, N_META, d))
    xp = jnp.concatenate([meta, x_prompt], axis=1)
    s0_p = jnp.zeros((bp, N_HEADS, HEAD_DIM, HEAD_DIM), f32)
    shift0_p = jnp.zeros((bp, d), f32)
    y_prompt, s_p, shift_p, kt_p, vt_p = _stack(xp, s0_p, shift0_p, None, None, w, tm=688, tt=344,
                                                skip=N_META, tm_last=512)
    frames_last = lambda a: jnp.swapaxes(a, 2, 3)
    ys, s_s, shift_s, kt_s, vt_s = _stack(x_sample, state_rwkv_wkv, state_rwkv_shift, frames_last(cache_sb_k),
                                          frames_last(cache_sb_v), w, tm=512, tt=64)
    return (y_prompt, ys, s_p, shift_p, frames_last(kt_p), frames_last(vt_p), s_s, shift_s,
            frames_last(kt_s), frames_last(vt_s))
```

```python
import functools

import jax
import jax.numpy as jnp
from jax import lax
from jax.experimental import pallas as pl
from jax.experimental.pallas import tpu as pltpu

D_MODEL = 1024
HEAD_DIM = 64
N_HEADS = D_MODEL // HEAD_DIM
RMS_EPS = 1e-6
GN_EPS = 64e-5

LANES = 128
CHUNK = 64
SCAN_CHUNKS_PER_STEP = 3
FF_CHUNK = 256
DEAD_LOG = -104.0
VMEM_LIMIT = 52 * 1024 * 1024

f32 = jnp.float32
bf16 = jnp.bfloat16

_NN = (((1,), (0,)), ((), ()))
_NT = (((1,), (1,)), ((), ()))
_TN = (((0,), (0,)), ((), ()))


def _mm(a, b, dims=_NN):
    return lax.dot_general(a.astype(bf16), b.astype(bf16), dims, preferred_element_type=f32)


def _split(a):
    hi = a.astype(bf16)
    lo = (a - hi.astype(f32)).astype(bf16)
    return hi, lo


def _mm3(a, b, dims=_NN):
    ah, al = _split(a)
    bh, bl = _split(b)
    dot = functools.partial(lax.dot_general, dimension_numbers=dims, preferred_element_type=f32)
    return dot(ah, bh) + (dot(ah, bl) + dot(al, bh))


def _mm_exact_rhs(a, b_bf16, dims=_NN):
    ah, al = _split(a)
    dot = functools.partial(lax.dot_general, dimension_numbers=dims, preferred_element_type=f32)
    return dot(ah, b_bf16) + dot(al, b_bf16)


def _mm_exact_lhs(a_bf16, b, dims=_NN):
    bh, bl = _split(b)
    dot = functools.partial(lax.dot_general, dimension_numbers=dims, preferred_element_type=f32)
    return dot(a_bf16, bh) + dot(a_bf16, bl)


def _rms(x, g):
    return x * lax.rsqrt(jnp.mean(x * x, axis=-1, keepdims=True) + RMS_EPS) * g


def _softplus(y):
    return jnp.maximum(y, 0.0) + jnp.log1p(jnp.exp(-jnp.abs(y)))


def _sigmoid(y):
    return 1.0 / (1.0 + jnp.exp(-y))


def _head_sum(x, eseg_ref, eexp_ref):
    return _mm_exact_rhs(x, eseg_ref[...])


def _head_expand(y, eexp_ref):
    return _mm_exact_rhs(y, eexp_ref[...])


def _full(shape):
    return pl.BlockSpec(shape, lambda *_: (0,) * len(shape))


def _resident(shape):
    return pl.BlockSpec(shape, lambda *_: (0,) * len(shape), pipeline_mode=pl.Buffered(1))


def _params(sem):
    return pltpu.CompilerParams(dimension_semantics=sem, vmem_limit_bytes=VMEM_LIMIT)


def _ffn_kernel(n_mix, n_head, *refs):
    mix, refs = refs[:n_mix], refs[n_mix:]
    head, (x_ref, g_ref, wg_ref, wu_ref, wo_ref, o_ref, acc_ref) = refs[:bool(n_head)], refs[bool(n_head):]
    rows = lambda ref: ref[...].reshape(ref.shape[-2:])
    x = rows(x_ref)
    if n_head:
        moved = jnp.concatenate([head[0][...], x[:x.shape[0] - n_head]], axis=0)
        x = jnp.where(pl.program_id(1) == 0, moved, x)
    if n_mix:
        y = rows(mix[0])
        if n_mix == 3:
            y = y * rows(mix[1])
        x = x + jnp.dot(y.astype(bf16), mix[-1][...], preferred_element_type=f32)
    h = _rms(x, g_ref[...]).astype(bf16)
    n_chunks = wg_ref.shape[1] // FF_CHUNK
    for c in range(n_chunks):
        sl = slice(c * FF_CHUNK, (c + 1) * FF_CHUNK)
        gate = jnp.dot(h, wg_ref[:, sl], preferred_element_type=f32)
        up = jnp.dot(h, wu_ref[:, sl], preferred_element_type=f32)
        act = (gate * _sigmoid(gate) * up).astype(bf16)
        part = jnp.dot(act, wo_ref[sl, :], preferred_element_type=f32)
        if c == 0:
            acc_ref[...] = part
        else:
            acc_ref[...] += part
    o_ref[...] = x + 0.5 * acc_ref[...]


def _ffn(x, g, wg, wu, wo, tm, mix=None, skip=0, head=None):
    b, t, d = x.shape
    fp = wg.shape[1]
    n_head = 0 if head is None else head.shape[0]
    assert not (n_head and (skip or mix is not None))
    t_out = t - skip + n_head
    assert t_out % tm == 0 and skip % 8 == 0 and n_head % 8 == 0 and tm % 8 == 0
    if skip or n_head:
        row_in = pl.BlockSpec(
            (pl.Element(1), pl.Element(tm), pl.Element(d)),
            lambda bi, i: (bi, pl.multiple_of(jnp.maximum(skip - n_head + i * tm, 0), 8), 0))
    else:
        row_in = pl.BlockSpec((None, tm, d), lambda bi, i: (bi, i, 0))
    row_out = pl.BlockSpec((None, tm, d), lambda bi, i: (bi, i, 0))
    mix_ins, mix_specs = [], []
    if mix is not None:
        y, gate, w_proj = mix
        mix_ins = [y] + ([gate] if gate is not None else []) + [w_proj]
        mix_specs = [row_in] * (len(mix_ins) - 1) + [_resident(w_proj.shape)]
    head_ins, head_specs = ([head], [_resident(head.shape)]) if n_head else ([], [])
    return pl.pallas_call(
        functools.partial(_ffn_kernel, len(mix_ins), n_head),
        grid=(b, t_out // tm),
        in_specs=(mix_specs + head_specs
                  + [row_in, _full((1, d)), _resident((d, fp)), _resident((d, fp)), _resident((fp, d))]),
        out_specs=row_out,
        out_shape=jax.ShapeDtypeStruct((b, t_out, d), f32),
        scratch_shapes=[pltpu.VMEM((tm, d), f32)],
        compiler_params=_params(("parallel", "parallel")),
        name="ffn",
    )(*mix_ins, *head_ins, x, g, wg, wu, wo)


def _rwkv_pre_kernel(x_ref, shift_ref, ng_ref, mu_ref, wrkv_ref, w0_ref, w1_ref, w2_ref,
                     a0_ref, a1_ref, a2_ref, g1_ref, g2_ref, kk_ref, ka_ref, eseg_ref, eexp_ref,
                     r_out, lw_out, k_out, v_out, kk_out, b_out, g_out, shift_out, prev_ref):
    i = pl.program_id(1)
    tt = x_ref.shape[1]

    @pl.when(i == 0)
    def _():
        prev_ref[...] = shift_ref[0]

    h = _rms(x_ref[0], ng_ref[...])
    first_row = lax.broadcasted_iota(jnp.int32, h.shape, 0) == 0
    h_prev = jnp.where(first_row, prev_ref[...], pltpu.roll(h, 1, axis=0))
    last = h[tt - 1:tt, :]
    prev_ref[...] = last
    shift_out[0] = last

    dx = h_prev - h
    xr, xw, xk, xv, xa, xg = ((h + dx * mu_ref[j:j + 1, :]).astype(bf16) for j in range(6))
    r = jnp.dot(xr, wrkv_ref[0], preferred_element_type=f32)
    k = jnp.dot(xk, wrkv_ref[1], preferred_element_type=f32)
    v = jnp.dot(xv, wrkv_ref[2], preferred_element_type=f32)
    lora_w = _mm(jnp.tanh(jnp.dot(xw, w1_ref[...], preferred_element_type=f32)), w2_ref[...])
    log_rate = -_softplus(-(w0_ref[...] + lora_w)) - 0.5
    a = _sigmoid(a0_ref[...] + _mm(jnp.dot(xa, a1_ref[...], preferred_element_type=f32), a2_ref[...]))
    g = _mm(_sigmoid(jnp.dot(xg, g1_ref[...], preferred_element_type=f32)), g2_ref[...])

    kk = k * kk_ref[...]
    norm = jnp.sqrt(_head_sum(kk * kk, eseg_ref, eexp_ref))
    kk = kk * _head_expand(1.0 / jnp.maximum(norm, 1e-12), eexp_ref)

    r_out[0] = r
    lw_out[0] = -jnp.exp(log_rate)
    k_out[0] = k * (1.0 + (a - 1.0) * ka_ref[...])
    v_out[0] = v
    kk_out[0] = kk
    b_out[0] = kk * a
    g_out[0] = g


def _rwkv_pre(x, shift0, ng, p, eseg, eexp, tt):
    b, t, d = x.shape
    seq = pl.BlockSpec((1, tt, d), lambda bi, i: (bi, i, 0))
    vec = pl.BlockSpec((1, 1, d), lambda bi, i: (bi, 0, 0))
    out = jax.ShapeDtypeStruct((b, t, d), f32)
    ins = [x, shift0, ng, p["mu"], p["w_rkv"], p["w0"], p["w1"], p["w2"], p["a0"], p["a1"], p["a2"],
           p["g1"], p["g2"], p["k_k"], p["k_a"], eseg, eexp]
    specs = [seq, vec] + [_resident(a.shape) for a in ins[2:]]
    return pl.pallas_call(
        _rwkv_pre_kernel,
        grid=(b, t // tt),
        in_specs=specs,
        out_specs=[seq] * 7 + [vec],
        out_shape=[out] * 7 + [jax.ShapeDtypeStruct((b, 1, d), f32)],
        scratch_shapes=[pltpu.VMEM((1, d), f32)],
        compiler_params=_params(("parallel", "arbitrary")),
        name="rwkv_pre",
    )(*ins)


def _rwkv_scan_kernel(t_total, n_sub, r_ref, lw_ref, k_ref, v_ref, kk_ref, b_ref, s0_ref, rk_ref, gnw_ref, gnb_ref,
                      tri_ref, y_ref, s_ref, sbd_ref):
    step = pl.program_id(1)
    C, N = CHUNK, HEAD_DIM
    per_tile = LANES // N
    pairs = range(N_HEADS // per_tile)
    tiles = [slice(p * LANES, (p + 1) * LANES) for p in pairs]

    @pl.when(step == 0)
    def _():
        zero = jnp.zeros((N, N), f32)
        for p in pairs:
            sbd_ref[p] = jnp.concatenate(
                [jnp.concatenate([s0_ref[0, per_tile * p], zero], axis=1),
                 jnp.concatenate([zero, s0_ref[0, per_tile * p + 1]], axis=1)], axis=0)

    row = lax.broadcasted_iota(jnp.int32, (C, 2 * C), 0)
    lane = lax.broadcasted_iota(jnp.int32, (C, 2 * C), 1)
    left = lane < C
    col = jnp.where(left, lane, lane - C)
    strict = row > col
    incl = row >= col
    eye_right = jnp.logical_and(row == col, jnp.logical_not(left)).astype(f32)
    head_lanes = [(lax.broadcasted_iota(jnp.int32, (2 * C, LANES), 1) // N) == s for s in range(per_tile)]
    first = lax.broadcasted_iota(jnp.int32, (C, LANES), 1) < N

    chunks = []
    for j in range(n_sub):
        rows = slice(j * C, (j + 1) * C)
        r, lw, k, v, kk, b = (ref[0, rows] for ref in (r_ref, lw_ref, k_ref, v_ref, kk_ref, b_ref))
        if t_total % (n_sub * C):
            valid = (step * n_sub + j) * C + lax.broadcasted_iota(jnp.int32, r.shape, 0) < t_total
            r, lw, k, v, kk, b = (jnp.where(valid, a, 0.0) for a in (r, lw, k, v, kk, b))
        cum = _mm_exact_lhs(tri_ref[...], lw)
        p_t = jnp.exp(cum)
        p_inv = jnp.exp(-cum)
        r_t = r * p_t
        a_t = -kk * jnp.exp(cum - lw)
        b_t = b * p_inv
        k_t = k * p_inv
        gamma = p_t[C - 1:C, :]
        b_g = b_t * gamma
        k_g = k_t * gamma
        chunks.append(dict(
            rows=rows, gamma=gamma, bonus=r * k * rk_ref[...],
            ar=[jnp.concatenate([a_t[:, sl], r_t[:, sl]], axis=0) for sl in tiles],
            bk=[jnp.concatenate([b_t[:, sl], k_t[:, sl]], axis=0) for sl in tiles],
            bkg=[jnp.concatenate([b_g[:, sl], k_g[:, sl]], axis=0) for sl in tiles],
            v=[v[:, sl] for sl in tiles]))

    chains = [(j, p, s) for j in range(n_sub) for p in pairs for s in range(per_tile)]
    ar = [jnp.where(head_lanes[s], chunks[j]["ar"][p], 0.0) for j, p, s in chains]
    x2 = [_mm(ar[i], chunks[j]["bk"][p], _NT) for i, (j, p, _) in enumerate(chains)]
    z = [jnp.where(jnp.logical_and(strict, left), x[:C], 0.0) + eye_right for x in x2]
    for _ in range(6):
        sq = [_mm(x[:, :C], x) for x in z]
        z = [jnp.where(left, q, x + q) for x, q in zip(z, sq)]
    t_inv = [jnp.where(left, 0.0, x) for x in z]
    a_ak = [jnp.where(jnp.logical_and(strict, jnp.logical_not(left)), x[:C], 0.0) for x in x2]
    akv = [_mm(a_ak[i], jnp.concatenate([chunks[j]["v"][p]] * 2, axis=0)) for i, (j, p, _) in enumerate(chains)]
    a_r = [jnp.where(incl, x[C:], 0.0) for x in x2]

    def head_sum(x):
        s0 = jnp.sum(jnp.where(first, x, 0.0), axis=-1, keepdims=True)
        s1 = jnp.sum(jnp.where(first, 0.0, x), axis=-1, keepdims=True)
        return jnp.where(first, s0, s1)

    for j in range(n_sub):
        ck = chunks[j]
        mine = [i for i, (jj, _, _) in enumerate(chains) if jj == j]
        s_old = [sbd_ref[p] for p in pairs]
        xs = {i: _mm_exact_lhs(ar[i].astype(bf16), s_old[chains[i][1]], _NT) for i in mine}
        rhs = {i: xs[i][:C] + akv[i] for i in mine}
        u = {i: _mm(t_inv[i], jnp.concatenate([rhs[i], rhs[i]], axis=0)) for i in mine}
        uv = {i: jnp.concatenate([u[i], ck["v"][chains[i][1]]], axis=0) for i in mine}
        o = {i: xs[i][C:] + _mm(a_r[i], uv[i]) for i in mine}
        s_add = {i: _mm(uv[i], jnp.where(head_lanes[chains[i][2]], ck["bkg"][chains[i][1]], 0.0), _TN) for i in mine}
        y_tiles = []
        for p in pairs:
            i0, sl = mine[per_tile * p], tiles[p]
            sbd_ref[p] = (s_old[p] * ck["gamma"][:, sl]
                          + jnp.concatenate([s_add[i0][:N], s_add[i0 + 1][N:]], axis=0))
            o_pair = jnp.where(first, o[i0], o[i0 + 1])
            dev = o_pair - head_sum(o_pair) * (1.0 / N)
            o_n = dev * lax.rsqrt(head_sum(dev * dev) * (1.0 / N) + GN_EPS) * gnw_ref[:, sl] + gnb_ref[:, sl]
            y_tiles.append(o_n + head_sum(ck["bonus"][:, sl]) * ck["v"][p])
        y_ref[0, ck["rows"]] = jnp.concatenate(y_tiles, axis=1)

    @pl.when(step == pl.num_programs(1) - 1)
    def _():
        for p in pairs:
            s_ref[0, per_tile * p] = sbd_ref[p][:N, :N]
            s_ref[0, per_tile * p + 1] = sbd_ref[p][N:, N:]


def _rwkv_scan(r, lw, k, v, kk, b, s0, p, tri):
    bsz, t, d = r.shape
    n_sub = min(SCAN_CHUNKS_PER_STEP, pl.cdiv(t, CHUNK))
    seq = pl.BlockSpec((1, n_sub * CHUNK, d), lambda bi, c: (bi, c, 0))
    st = pl.BlockSpec((1, N_HEADS, HEAD_DIM, HEAD_DIM), lambda bi, c: (bi, 0, 0, 0))
    consts = [p["r_k"], p["gn_w"], p["gn_b"], tri]
    return pl.pallas_call(
        functools.partial(_rwkv_scan_kernel, t, n_sub),
        grid=(bsz, pl.cdiv(t, n_sub * CHUNK)),
        in_specs=[seq] * 6 + [st] + [_resident(a.shape) for a in consts],
        out_specs=[seq, st],
        out_shape=[jax.ShapeDtypeStruct((bsz, t, d), f32), jax.ShapeDtypeStruct(s0.shape, f32)],
        scratch_shapes=[pltpu.VMEM((N_HEADS * HEAD_DIM // LANES, LANES, LANES), f32)],
        compiler_params=_params(("parallel", "arbitrary")),
        name="rwkv_scan",
    )(r, lw, k, v, kk, b, s0, *consts)


SB_PRE_HEADS = 4


def _sb_pre_kernel(reuse_h, x_ref, ng_ref, wq_ref, wkv_ref, qg_ref, kg_ref, eseg_ref, eexp_ref, q_out, kt_out, vt_out,
                   h_ref):
    hg, n = SB_PRE_HEADS, HEAD_DIM
    t = x_ref.shape[1]

    def normed():
        return _rms(x_ref[0], ng_ref[...]).astype(bf16)

    if reuse_h:
        @pl.when(pl.program_id(1) == 0)
        def _():
            h_ref[...] = normed()
        h = h_ref[...]
    else:
        h = normed()
    q = jnp.dot(h, wq_ref[...], preferred_element_type=f32)
    ms = _head_sum(q * q, eseg_ref, eexp_ref) * (1.0 / n)
    q_out[0] = q * _head_expand(lax.rsqrt(ms + RMS_EPS), eexp_ref) * qg_ref[...]

    kv = lax.dot_general(wkv_ref[0], h, _NT, preferred_element_type=f32)
    k = kv[:hg * n].reshape(hg, n, t)
    ms = jnp.mean(k * k, axis=1, keepdims=True)
    kt_out[0] = k * lax.rsqrt(ms + RMS_EPS) * kg_ref[...]
    vt_out[0] = kv[hg * n:].reshape(hg, n, t)


def _sb_pre(x, ng, wq, wkv_t, qg, kg_col, eseg, eexp, streams_inner):
    b, t, d = x.shape
    hg, n = SB_PRE_HEADS, HEAD_DIM
    w = hg * n
    ins = [x, ng, wq, wkv_t, qg, kg_col, eseg[:w], eexp[:, :w]]
    if streams_inner:
        grid = (N_HEADS // hg, b)
        ix = lambda f: (lambda g, bi: f(bi, g))
    else:
        grid = (b, N_HEADS // hg)
        ix = lambda f: f
    tspec = pl.BlockSpec((1, hg, n, t), ix(lambda bi, g: (bi, g, 0, 0)))
    tout = jax.ShapeDtypeStruct((b, N_HEADS, n, t), f32)
    return pl.pallas_call(
        functools.partial(_sb_pre_kernel, not streams_inner),
        grid=grid,
        in_specs=[
            pl.BlockSpec((1, t, d), ix(lambda bi, g: (bi, 0, 0)),
                         pipeline_mode=None if streams_inner else pl.Buffered(1)),
            _resident(ng.shape),
            pl.BlockSpec((d, w), ix(lambda bi, g: (0, g))),
            pl.BlockSpec((1, 2 * w, d), ix(lambda bi, g: (g, 0, 0))),
            pl.BlockSpec((1, w), ix(lambda bi, g: (0, g))),
            _resident(kg_col.shape),
            _resident((w, LANES)),
            _resident((LANES, w)),
        ],
        out_specs=[pl.BlockSpec((1, t, w), ix(lambda bi, g: (bi, 0, g))), tspec, tspec],
        out_shape=[jax.ShapeDtypeStruct((b, t, d), f32), tout, tout],
        scratch_shapes=[pltpu.VMEM((t, d), bf16)],
        compiler_params=_params(("arbitrary", "arbitrary")),
        name="sb_pre",
    )(*ins)


SB_QBLOCK = 128
SB_WINDOW = 3 * SB_QBLOCK
SB_QBLOCKS_PER_STEP = 4
SB_PAST_WINDOW = 256
SB_SAMPLE_HEADS = 8


def _later_matrix(n):
    i = jnp.arange(n)
    return (i[:, None] > i[None, :]).astype(bf16)


def _sb_windows(qs, kts, vts, tri, carries, masks):
    n = range(len(qs))
    z = [_mm(qs[i], kts[i]) * (HEAD_DIM ** -0.5) for i in n]
    log_beta = [jnp.minimum(x, 0.0) - jnp.log(1.0 + jnp.exp(-jnp.abs(x))) for x in z]
    log_1m = [log_beta[i] - z[i] for i in n]
    log_1m = [x if m is None else jnp.where(m, x, 0.0) for x, m in zip(log_1m, masks)]
    later = [_mm(x, tri) for x in log_1m]
    a = [jnp.exp(log_beta[i] + later[i] + carries[i]) for i in n]
    a = [x if m is None else jnp.where(m, x, 0.0) for x, m in zip(a, masks)]
    outs = [_mm(a[i], vts[i], _NT) for i in n]
    return outs, [carries[i] + later[i][:, :1] + log_1m[i][:, :1] for i in n]


def _any_alive(kends, carries):
    flags = [jnp.logical_and(k > 0, jnp.max(c) > DEAD_LOG) for k, c in zip(kends, carries)]
    return functools.reduce(jnp.logical_or, flags)


def _sb_prompt_kernel(q_ref, kt_ref, vt_ref, tri_ref, o_ref):
    step = pl.program_id(2)
    t = q_ref.shape[1]
    N, Q, W, G = HEAD_DIM, SB_QBLOCK, SB_WINDOW, SB_QBLOCKS_PER_STEP
    heads = range(LANES // N)
    tri = tri_ref[...]

    def window(ws, hd):
        return kt_ref[0, hd, :, pl.ds(ws, W)], vt_ref[0, hd, :, pl.ds(ws, W)]

    def earlier_windows(qs, chain_heads, kends, accs, carries):
        col = lax.broadcasted_iota(jnp.int32, (qs[0].shape[0], W), 1)

        def body(st):
            kends, accs, carries = st
            ws = [pl.multiple_of(jnp.maximum(k - W, 0), Q) for k in kends]
            kts, vts = zip(*[window(s, hd) for s, hd in zip(ws, chain_heads)])
            masks = [s + col < k for s, k in zip(ws, kends)]
            parts, carries = _sb_windows(qs, kts, vts, tri, carries, masks)
            return ws, [a + p for a, p in zip(accs, parts)], carries

        return lax.while_loop(lambda st: _any_alive(st[0], st[2]), body, (kends, accs, carries))[1]

    row = lax.broadcasted_iota(jnp.int32, (Q, W), 0)
    col = lax.broadcasted_iota(jnp.int32, (Q, W), 1)
    q0 = [pl.multiple_of((step * G + j) * Q, Q) for j in range(G)]
    ws = [pl.multiple_of(jnp.maximum(x - (W - Q), 0), Q) for x in q0]
    chains = [(j, hd) for j in range(G) for hd in heads]
    chain_heads = [hd for _, hd in chains]
    qs = [q_ref[0, pl.ds(q0[j], Q), hd * N:(hd + 1) * N] for j, hd in chains]
    kts, vts = zip(*[window(ws[j], hd) for j, hd in chains])
    masks = [ws[j] + col < q0[j] + row for j, _ in chains]
    zero = jnp.zeros((Q, 1), f32)
    accs, carries = _sb_windows(qs, kts, vts, tri, [zero] * len(chains), masks)
    accs = earlier_windows(qs, chain_heads, [ws[j] for j, _ in chains], accs, carries)
    for j in range(G):
        o_ref[0, pl.ds(q0[j], Q), :] = jnp.concatenate([accs[j * len(heads) + hd] for hd in heads], axis=1)

    q_full = (t // Q) * Q
    rem = t - q_full
    if rem:
        @pl.when(step == pl.num_programs(2) - 1)
        def _():
            qs = [q_ref[0, q_full:, hd * N:(hd + 1) * N] for hd in heads]
            kts = [kt_ref[0, hd, :, q_full:] for hd in heads]
            vts = [vt_ref[0, hd, :, q_full:] for hd in heads]
            causal = (lax.broadcasted_iota(jnp.int32, (rem, rem), 1)
                      < lax.broadcasted_iota(jnp.int32, (rem, rem), 0))
            zero = jnp.zeros((rem, 1), f32)
            accs, carries = _sb_windows(qs, kts, vts, tri[:rem, :rem], [zero] * len(heads), [causal] * len(heads))
            accs = earlier_windows(qs, list(heads), [jnp.int32(q_full)] * len(heads), accs, carries)
            o_ref[0, q_full:, :] = jnp.concatenate(accs, axis=1)


def _sb_prompt(q, kt, vt):
    b, t, d = q.shape
    n_full = t // SB_QBLOCK
    assert n_full % SB_QBLOCKS_PER_STEP == 0 and t >= SB_WINDOW
    tri = _later_matrix(SB_WINDOW)
    qspec = pl.BlockSpec((1, t, LANES), lambda bi, hp, s: (bi, 0, hp))
    kvspec = pl.BlockSpec((1, LANES // HEAD_DIM, HEAD_DIM, t), lambda bi, hp, s: (bi, hp, 0, 0))
    return pl.pallas_call(
        _sb_prompt_kernel,
        grid=(b, d // LANES, n_full // SB_QBLOCKS_PER_STEP),
        in_specs=[qspec, kvspec, kvspec, _resident(tri.shape)],
        out_specs=qspec,
        out_shape=jax.ShapeDtypeStruct((b, t, d), f32),
        compiler_params=_params(("parallel", "parallel", "arbitrary")),
        name="sb_prompt",
    )(q, kt, vt, tri)


def _sb_sample_kernel(q_ref, ktn_ref, vtn_ref, ktp_ref, vtp_ref, kt_hbm, vt_hbm, tri_ref, o_ref, kbuf, vbuf, sem):
    bi, g = pl.program_id(0), pl.program_id(1)
    N = HEAD_DIM
    hg, t, wp = ktn_ref.shape[1], q_ref.shape[1], ktp_ref.shape[3]
    heads = range(hg)
    tri = tri_ref[...]

    qs = [q_ref[0, :, hd * N:(hd + 1) * N] for hd in heads]
    kts = [jnp.concatenate([ktp_ref[0, hd], ktn_ref[0, hd]], axis=1) for hd in heads]
    vts = [jnp.concatenate([vtp_ref[0, hd], vtn_ref[0, hd]], axis=1) for hd in heads]
    row = lax.broadcasted_iota(jnp.int32, (t, wp + t), 0)
    col = lax.broadcasted_iota(jnp.int32, (t, wp + t), 1)
    mask = col - wp < row
    accs, carries = _sb_windows(qs, kts, vts, tri, [jnp.zeros((t, 1), f32)] * hg, [mask] * hg)

    def fetch(src, dst, ws, slot):
        return pltpu.make_async_copy(src.at[bi, pl.ds(g * hg, hg), :, pl.ds(ws, wp)], dst, sem.at[slot])

    def body(st):
        kend, accs, carries = st
        ws = pl.multiple_of(kend - wp, wp)
        copies = [fetch(kt_hbm, kbuf, ws, 0), fetch(vt_hbm, vbuf, ws, 1)]
        for c in copies:
            c.start()
        for c in copies:
            c.wait()
        parts, carries = _sb_windows(qs, [kbuf[hd] for hd in heads], [vbuf[hd] for hd in heads],
                                     tri[:wp, :wp], carries, [None] * hg)
        return ws, [a + p for a, p in zip(accs, parts)], carries

    kend = jnp.int32(kt_hbm.shape[3] - wp)
    accs = lax.while_loop(lambda st: _any_alive([st[0]] * hg, st[2]), body, (kend, accs, carries))[1]
    o_ref[0] = jnp.concatenate(accs, axis=1)


def _sb_sample(q, kt, vt, kt_past, vt_past):
    b, t, d = q.shape
    hg, n, wp = SB_SAMPLE_HEADS, HEAD_DIM, SB_PAST_WINDOW
    p = kt_past.shape[3]
    assert p % wp == 0
    tri = _later_matrix(wp + t)
    qspec = pl.BlockSpec((1, t, hg * n), lambda bi, g: (bi, 0, g))
    new = pl.BlockSpec((1, hg, n, t), lambda bi, g: (bi, g, 0, 0))
    last = pl.BlockSpec((1, hg, n, wp), lambda bi, g: (bi, g, 0, p // wp - 1))
    hbm = pl.BlockSpec(memory_space=pl.ANY)
    return pl.pallas_call(
        _sb_sample_kernel,
        grid=(b, N_HEADS // hg),
        in_specs=[qspec, new, new, last, last, hbm, hbm, _resident(tri.shape)],
        out_specs=qspec,
        out_shape=jax.ShapeDtypeStruct((b, t, d), f32),
        scratch_shapes=[pltpu.VMEM((hg, n, wp), f32), pltpu.VMEM((hg, n, wp), f32), pltpu.SemaphoreType.DMA((2,))],
        compiler_params=_params(("parallel", "parallel")),
        name="sb_sample",
    )(q, kt, vt, kt_past, vt_past, kt_past, vt_past, tri)


def _grouped_kv_t(w_kv):
    d = w_kv.shape[0]
    gw = SB_PRE_HEADS * HEAD_DIM
    k_t = w_kv[:, :d].T.reshape(d // gw, gw, d)
    v_t = w_kv[:, d:].T.reshape(d // gw, gw, d)
    return jnp.concatenate([k_t, v_t], axis=1)


def _stack(x, s0, shift0, kt_past, vt_past, w, tm, tt, head=None, tm_last=None):
    b, t0, d = x.shape
    skip = 0 if head is None else head.shape[0]
    t = t0 + skip
    m = b * t
    flat = lambda a: a.reshape(1, m, d)

    def ffn(x2, i, j, mix=None):
        if mix is not None:
            mix = (flat(mix[0]), None if mix[1] is None else flat(mix[1]), mix[2])
        return _ffn(flat(x2), w["ffn_g"][i][j], w["ffn_wg"][i][j], w["ffn_wu"][i][j], w["ffn_wo"][i][j],
                    tm, mix).reshape(m, d)

    if head is None:
        x2 = ffn(x.reshape(m, d), 0, 0)
    else:
        x2 = _ffn(x, w["ffn_g"][0][0], w["ffn_wg"][0][0], w["ffn_wu"][0][0], w["ffn_wo"][0][0], tm,
                  head=head).reshape(m, d)
    rw = w["rwkv"]
    r, lw, k, v, kk, bb, g, shift = _rwkv_pre(x2.reshape(b, t, d), shift0.reshape(b, 1, d), w["mix_g"][0], rw,
                                              w["eseg"], w["eexp"], tt)
    y, s_fin = _rwkv_scan(r, lw, k, v, kk, bb, s0, rw, w["tri_chunk"])
    x2 = ffn(x2, 0, 1, (y, g, rw["w_o"]))

    x2 = ffn(x2, 1, 0)
    sb = w["sb"]
    q, kt, vt = _sb_pre(x2.reshape(b, t, d), w["mix_g"][1], sb["w_q"], sb["w_kv_t"], sb["q_g"], sb["k_g_col"],
                        w["eseg"], w["eexp"], streams_inner=4 * t < 6 * SB_PRE_HEADS * HEAD_DIM)
    if kt_past is None:
        o = _sb_prompt(q, kt, vt)
    else:
        o = _sb_sample(q, kt, vt, kt_past, vt_past)
    if skip:
        y = _ffn(x2.reshape(b, t, d), w["ffn_g"][1][1], w["ffn_wg"][1][1], w["ffn_wu"][1][1], w["ffn_wo"][1][1],
                 tm_last, (o, None, sb["w_o"]), skip)
    else:
        y = ffn(x2, 1, 1, (o, None, sb["w_o"])).reshape(b, t, d)
    return y, s_fin, shift.reshape(b, d), kt, vt


def kernel(x_prompt, x_sample, state_rwkv_wkv, state_rwkv_shift, cache_sb_k, cache_sb_v, meta_tokens,
           ffn_norm_g, ffn_w_in, ffn_w_out, mix_norm_g, rwkv_mu, rwkv_w_rkv, rwkv_w0, rwkv_w1, rwkv_w2,
           rwkv_a0, rwkv_a1, rwkv_a2, rwkv_g1, rwkv_g2, rwkv_k_k, rwkv_k_a, rwkv_r_k, rwkv_gn_w, rwkv_gn_b,
           rwkv_w_o, sb_w_qkv, sb_q_norm_g, sb_k_norm_g, sb_w_o):
    d = D_MODEL
    depth = ffn_w_in.shape[0]
    d_ff = ffn_w_out.shape[2]
    pad = (-d_ff) % FF_CHUNK
    w_gate = jnp.pad(ffn_w_in[..., :d_ff], ((0, 0), (0, 0), (0, 0), (0, pad))).astype(bf16)
    w_up = jnp.pad(ffn_w_in[..., d_ff:], ((0, 0), (0, 0), (0, 0), (0, pad))).astype(bf16)
    w_out = jnp.pad(ffn_w_out, ((0, 0), (0, 0), (0, pad), (0, 0))).astype(bf16)
    vec = lambda a: a.reshape(1, d).astype(f32)
    head_of_lane = jnp.arange(d) // HEAD_DIM
    eseg = (head_of_lane[:, None] == jnp.arange(LANES)[None, :]).astype(bf16)
    w = {
        "ffn_g": [[vec(ffn_norm_g[i, j]) for j in range(2)] for i in range(depth)],
        "ffn_wg": w_gate, "ffn_wu": w_up, "ffn_wo": w_out,
        "mix_g": [vec(mix_norm_g[i]) for i in range(depth)],
        "eseg": eseg, "eexp": eseg.T,
        "tri_chunk": (jnp.arange(CHUNK)[:, None] >= jnp.arange(CHUNK)[None, :]).astype(bf16),
        "rwkv": {
            "mu": rwkv_mu, "w_rkv": rwkv_w_rkv.astype(bf16), "w0": vec(rwkv_w0), "w1": rwkv_w1.astype(bf16),
            "w2": rwkv_w2.astype(bf16), "a0": vec(rwkv_a0), "a1": rwkv_a1.astype(bf16), "a2": rwkv_a2.astype(bf16),
            "g1": rwkv_g1.astype(bf16), "g2": rwkv_g2.astype(bf16), "k_k": vec(rwkv_k_k), "k_a": vec(rwkv_k_a),
            "r_k": vec(rwkv_r_k), "gn_w": vec(rwkv_gn_w), "gn_b": vec(rwkv_gn_b), "w_o": rwkv_w_o.astype(bf16),
        },
        "sb": {
            "w_q": sb_w_qkv[:, :d].astype(bf16), "w_kv_t": _grouped_kv_t(sb_w_qkv[:, d:]).astype(bf16),
            "q_g": vec(jnp.tile(sb_q_norm_g, N_HEADS)), "k_g_col": sb_k_norm_g.reshape(HEAD_DIM, 1).astype(f32),
            "w_o": sb_w_o.astype(bf16),
        },
    }

    bp = x_prompt.shape[0]
    s0_p = jnp.zeros((bp, N_HEADS, HEAD_DIM, HEAD_DIM), f32)
    shift0_p = jnp.zeros((bp, d), f32)
    y_prompt, s_p, shift_p, kt_p, vt_p = _stack(x_prompt, s0_p, shift0_p, None, None, w, tm=688, tt=344,
                                                head=meta_tokens.astype(f32), tm_last=512)
    frames_last = lambda a: jnp.swapaxes(a, 2, 3)
    ys, s_s, shift_s, kt_s, vt_s = _stack(x_sample, state_rwkv_wkv, state_rwkv_shift, frames_last(cache_sb_k),
                                          frames_last(cache_sb_v), w, tm=512, tt=64)
    return (y_prompt, ys, s_p, shift_p, frames_last(kt_p), frames_last(vt_p), s_s, shift_s,
            frames_last(kt_s), frames_last(vt_s))
```

```python
import functools

import jax
import jax.numpy as jnp
from jax import lax
from jax.experimental import pallas as pl
from jax.experimental.pallas import tpu as pltpu

D_MODEL = 1024
HEAD_DIM = 64
N_HEADS = D_MODEL // HEAD_DIM
RMS_EPS = 1e-6
GN_EPS = 64e-5

LANES = 128
CHUNK = 64
SCAN_CHUNKS_PER_STEP = 3
FF_CHUNK = 256
DEAD_LOG = -104.0
VMEM_LIMIT = 52 * 1024 * 1024

f32 = jnp.float32
bf16 = jnp.bfloat16

_NN = (((1,), (0,)), ((), ()))
_NT = (((1,), (1,)), ((), ()))
_TN = (((0,), (0,)), ((), ()))


def _mm(a, b, dims=_NN):
    return lax.dot_general(a.astype(bf16), b.astype(bf16), dims, preferred_element_type=f32)


def _split(a):
    hi = a.astype(bf16)
    lo = (a - hi.astype(f32)).astype(bf16)
    return hi, lo


def _mm3(a, b, dims=_NN):
    ah, al = _split(a)
    bh, bl = _split(b)
    dot = functools.partial(lax.dot_general, dimension_numbers=dims, preferred_element_type=f32)
    return dot(ah, bh) + (dot(ah, bl) + dot(al, bh))


def _mm_exact_rhs(a, b_bf16, dims=_NN):
    ah, al = _split(a)
    dot = functools.partial(lax.dot_general, dimension_numbers=dims, preferred_element_type=f32)
    return dot(ah, b_bf16) + dot(al, b_bf16)


def _mm_exact_lhs(a_bf16, b, dims=_NN):
    bh, bl = _split(b)
    dot = functools.partial(lax.dot_general, dimension_numbers=dims, preferred_element_type=f32)
    return dot(a_bf16, bh) + dot(a_bf16, bl)


def _rms(x, g):
    return x * lax.rsqrt(jnp.mean(x * x, axis=-1, keepdims=True) + RMS_EPS) * g


def _softplus(y):
    return jnp.maximum(y, 0.0) + jnp.log1p(jnp.exp(-jnp.abs(y)))


def _sigmoid(y):
    return 1.0 / (1.0 + jnp.exp(-y))


def _head_sum(x, eseg_ref, eexp_ref):
    return _mm_exact_rhs(x, eseg_ref[...])


def _head_expand(y, eexp_ref):
    return _mm_exact_rhs(y, eexp_ref[...])


def _full(shape):
    return pl.BlockSpec(shape, lambda *_: (0,) * len(shape))


def _resident(shape):
    return pl.BlockSpec(shape, lambda *_: (0,) * len(shape), pipeline_mode=pl.Buffered(1))


def _params(sem):
    return pltpu.CompilerParams(dimension_semantics=sem, vmem_limit_bytes=VMEM_LIMIT)


def _ffn_kernel(n_mix, n_head, *refs):
    mix, refs = refs[:n_mix], refs[n_mix:]
    head, (x_ref, g_ref, wg_ref, wu_ref, wo_ref, o_ref, acc_ref) = refs[:bool(n_head)], refs[bool(n_head):]
    rows = lambda ref: ref[...].reshape(ref.shape[-2:])
    x = rows(x_ref)
    if n_head:
        moved = jnp.concatenate([head[0][...], x[:x.shape[0] - n_head]], axis=0)
        x = jnp.where(pl.program_id(1) == 0, moved, x)
    if n_mix:
        y = rows(mix[0])
        if n_mix == 3:
            y = y * rows(mix[1])
        x = x + jnp.dot(y.astype(bf16), mix[-1][...], preferred_element_type=f32)
    h = _rms(x, g_ref[...]).astype(bf16)
    n_chunks = wg_ref.shape[1] // FF_CHUNK
    for c in range(n_chunks):
        sl = slice(c * FF_CHUNK, (c + 1) * FF_CHUNK)
        gate = jnp.dot(h, wg_ref[:, sl], preferred_element_type=f32)
        up = jnp.dot(h, wu_ref[:, sl], preferred_element_type=f32)
        act = (gate * _sigmoid(gate) * up).astype(bf16)
        part = jnp.dot(act, wo_ref[sl, :], preferred_element_type=f32)
        if c == 0:
            acc_ref[...] = part
        else:
            acc_ref[...] += part
    o_ref[...] = x + 0.5 * acc_ref[...]


def _ffn(x, g, wg, wu, wo, ij, tm, mix=None, skip=0, head=None):
    b, t, d = x.shape
    fp = wg.shape[-1]
    picked = lambda *shape: pl.BlockSpec((None, None) + shape, lambda *_: (*ij, 0, 0), pipeline_mode=pl.Buffered(1))
    n_head = 0 if head is None else head.shape[0]
    assert not (n_head and (skip or mix is not None))
    t_out = t - skip + n_head
    assert t_out % tm == 0 and skip % 8 == 0 and n_head % 8 == 0 and tm % 8 == 0
    if skip or n_head:
        row_in = pl.BlockSpec(
            (pl.Element(1), pl.Element(tm), pl.Element(d)),
            lambda bi, i: (bi, pl.multiple_of(jnp.maximum(skip - n_head + i * tm, 0), 8), 0))
    else:
        row_in = pl.BlockSpec((None, tm, d), lambda bi, i: (bi, i, 0))
    row_out = pl.BlockSpec((None, tm, d), lambda bi, i: (bi, i, 0))
    mix_ins, mix_specs = [], []
    if mix is not None:
        y, gate, w_proj = mix
        mix_ins = [y] + ([gate] if gate is not None else []) + [w_proj]
        mix_specs = [row_in] * (len(mix_ins) - 1) + [_resident(w_proj.shape)]
    head_ins, head_specs = ([head], [_resident(head.shape)]) if n_head else ([], [])
    return pl.pallas_call(
        functools.partial(_ffn_kernel, len(mix_ins), n_head),
        grid=(b, t_out // tm),
        in_specs=(mix_specs + head_specs
                  + [row_in, _full((1, d)), picked(d, fp), picked(d, fp), picked(fp, d)]),
        out_specs=row_out,
        out_shape=jax.ShapeDtypeStruct((b, t_out, d), f32),
        scratch_shapes=[pltpu.VMEM((tm, d), f32)],
        compiler_params=_params(("parallel", "parallel")),
        name="ffn",
    )(*mix_ins, *head_ins, x, g, wg, wu, wo)


def _rwkv_pre_kernel(x_ref, shift_ref, ng_ref, mu_ref, wrkv_ref, w0_ref, w1_ref, w2_ref,
                     a0_ref, a1_ref, a2_ref, g1_ref, g2_ref, kk_ref, ka_ref, eseg_ref, eexp_ref,
                     r_out, lw_out, k_out, v_out, kk_out, b_out, g_out, shift_out, prev_ref):
    i = pl.program_id(1)
    tt = x_ref.shape[1]

    @pl.when(i == 0)
    def _():
        prev_ref[...] = shift_ref[0]

    h = _rms(x_ref[0], ng_ref[...])
    first_row = lax.broadcasted_iota(jnp.int32, h.shape, 0) == 0
    h_prev = jnp.where(first_row, prev_ref[...], pltpu.roll(h, 1, axis=0))
    last = h[tt - 1:tt, :]
    prev_ref[...] = last
    shift_out[0] = last

    dx = h_prev - h
    xr, xw, xk, xv, xa, xg = ((h + dx * mu_ref[j:j + 1, :]).astype(bf16) for j in range(6))
    r = jnp.dot(xr, wrkv_ref[0], preferred_element_type=f32)
    k = jnp.dot(xk, wrkv_ref[1], preferred_element_type=f32)
    v = jnp.dot(xv, wrkv_ref[2], preferred_element_type=f32)
    lora_w = _mm(jnp.tanh(jnp.dot(xw, w1_ref[...], preferred_element_type=f32)), w2_ref[...])
    log_rate = -_softplus(-(w0_ref[...] + lora_w)) - 0.5
    a = _sigmoid(a0_ref[...] + _mm(jnp.dot(xa, a1_ref[...], preferred_element_type=f32), a2_ref[...]))
    g = _mm(_sigmoid(jnp.dot(xg, g1_ref[...], preferred_element_type=f32)), g2_ref[...])

    kk = k * kk_ref[...]
    norm = jnp.sqrt(_head_sum(kk * kk, eseg_ref, eexp_ref))
    kk = kk * _head_expand(1.0 / jnp.maximum(norm, 1e-12), eexp_ref)

    r_out[0] = r
    lw_out[0] = -jnp.exp(log_rate)
    k_out[0] = k * (1.0 + (a - 1.0) * ka_ref[...])
    v_out[0] = v
    kk_out[0] = kk
    b_out[0] = kk * a
    g_out[0] = g


def _rwkv_pre(x, shift0, ng, p, eseg, eexp, tt):
    b, t, d = x.shape
    seq = pl.BlockSpec((1, tt, d), lambda bi, i: (bi, i, 0))
    vec = pl.BlockSpec((1, 1, d), lambda bi, i: (bi, 0, 0))
    out = jax.ShapeDtypeStruct((b, t, d), f32)
    ins = [x, shift0, ng, p["mu"], p["w_rkv"], p["w0"], p["w1"], p["w2"], p["a0"], p["a1"], p["a2"],
           p["g1"], p["g2"], p["k_k"], p["k_a"], eseg, eexp]
    specs = [seq, vec] + [_resident(a.shape) for a in ins[2:]]
    return pl.pallas_call(
        _rwkv_pre_kernel,
        grid=(b, t // tt),
        in_specs=specs,
        out_specs=[seq] * 7 + [vec],
        out_shape=[out] * 7 + [jax.ShapeDtypeStruct((b, 1, d), f32)],
        scratch_shapes=[pltpu.VMEM((1, d), f32)],
        compiler_params=_params(("parallel", "arbitrary")),
        name="rwkv_pre",
    )(*ins)


def _rwkv_scan_kernel(t_total, n_sub, r_ref, lw_ref, k_ref, v_ref, kk_ref, b_ref, s0_ref, rk_ref, gnw_ref, gnb_ref,
                      tri_ref, y_ref, s_ref, sbd_ref):
    step = pl.program_id(1)
    C, N = CHUNK, HEAD_DIM
    per_tile = LANES // N
    pairs = range(N_HEADS // per_tile)
    tiles = [slice(p * LANES, (p + 1) * LANES) for p in pairs]

    @pl.when(step == 0)
    def _():
        zero = jnp.zeros((N, N), f32)
        for p in pairs:
            sbd_ref[p] = jnp.concatenate(
                [jnp.concatenate([s0_ref[0, per_tile * p], zero], axis=1),
                 jnp.concatenate([zero, s0_ref[0, per_tile * p + 1]], axis=1)], axis=0)

    row = lax.broadcasted_iota(jnp.int32, (C, 2 * C), 0)
    lane = lax.broadcasted_iota(jnp.int32, (C, 2 * C), 1)
    left = lane < C
    col = jnp.where(left, lane, lane - C)
    strict = row > col
    incl = row >= col
    eye_right = jnp.logical_and(row == col, jnp.logical_not(left)).astype(f32)
    head_lanes = [(lax.broadcasted_iota(jnp.int32, (2 * C, LANES), 1) // N) == s for s in range(per_tile)]
    first = lax.broadcasted_iota(jnp.int32, (C, LANES), 1) < N

    chunks = []
    for j in range(n_sub):
        rows = slice(j * C, (j + 1) * C)
        r, lw, k, v, kk, b = (ref[0, rows] for ref in (r_ref, lw_ref, k_ref, v_ref, kk_ref, b_ref))
        if t_total % (n_sub * C):
            valid = (step * n_sub + j) * C + lax.broadcasted_iota(jnp.int32, r.shape, 0) < t_total
            r, lw, k, v, kk, b = (jnp.where(valid, a, 0.0) for a in (r, lw, k, v, kk, b))
        cum = _mm_exact_lhs(tri_ref[...], lw)
        p_t = jnp.exp(cum)
        p_inv = jnp.exp(-cum)
        r_t = r * p_t
        a_t = -kk * jnp.exp(cum - lw)
        b_t = b * p_inv
        k_t = k * p_inv
        gamma = p_t[C - 1:C, :]
        b_g = b_t * gamma
        k_g = k_t * gamma
        chunks.append(dict(
            rows=rows, gamma=gamma, bonus=r * k * rk_ref[...],
            ar=[jnp.concatenate([a_t[:, sl], r_t[:, sl]], axis=0) for sl in tiles],
            bk=[jnp.concatenate([b_t[:, sl], k_t[:, sl]], axis=0) for sl in tiles],
            bkg=[jnp.concatenate([b_g[:, sl], k_g[:, sl]], axis=0) for sl in tiles],
            v=[v[:, sl] for sl in tiles]))

    chains = [(j, p, s) for j in range(n_sub) for p in pairs for s in range(per_tile)]
    ar = [jnp.where(head_lanes[s], chunks[j]["ar"][p], 0.0) for j, p, s in chains]
    x2 = [_mm(ar[i], chunks[j]["bk"][p], _NT) for i, (j, p, _) in enumerate(chains)]
    z = [jnp.where(jnp.logical_and(strict, left), x[:C], 0.0) + eye_right for x in x2]
    for _ in range(6):
        sq = [_mm(x[:, :C], x) for x in z]
        z = [jnp.where(left, q, x + q) for x, q in zip(z, sq)]
    t_inv = [jnp.where(left, 0.0, x) for x in z]
    a_ak = [jnp.where(jnp.logical_and(strict, jnp.logical_not(left)), x[:C], 0.0) for x in x2]
    akv = [_mm(a_ak[i], jnp.concatenate([chunks[j]["v"][p]] * 2, axis=0)) for i, (j, p, _) in enumerate(chains)]
    a_r = [jnp.where(incl, x[C:], 0.0) for x in x2]

    def head_sum(x):
        s0 = jnp.sum(jnp.where(first, x, 0.0), axis=-1, keepdims=True)
        s1 = jnp.sum(jnp.where(first, 0.0, x), axis=-1, keepdims=True)
        return jnp.where(first, s0, s1)

    for j in range(n_sub):
        ck = chunks[j]
        mine = [i for i, (jj, _, _) in enumerate(chains) if jj == j]
        s_old = [sbd_ref[p] for p in pairs]
        xs = {i: _mm_exact_lhs(ar[i].astype(bf16), s_old[chains[i][1]], _NT) for i in mine}
        rhs = {i: xs[i][:C] + akv[i] for i in mine}
        u = {i: _mm(t_inv[i], jnp.concatenate([rhs[i], rhs[i]], axis=0)) for i in mine}
        uv = {i: jnp.concatenate([u[i], ck["v"][chains[i][1]]], axis=0) for i in mine}
        o = {i: xs[i][C:] + _mm(a_r[i], uv[i]) for i in mine}
        s_add = {i: _mm(uv[i], jnp.where(head_lanes[chains[i][2]], ck["bkg"][chains[i][1]], 0.0), _TN) for i in mine}
        y_tiles = []
        for p in pairs:
            i0, sl = mine[per_tile * p], tiles[p]
            sbd_ref[p] = (s_old[p] * ck["gamma"][:, sl]
                          + jnp.concatenate([s_add[i0][:N], s_add[i0 + 1][N:]], axis=0))
            o_pair = jnp.where(first, o[i0], o[i0 + 1])
            dev = o_pair - head_sum(o_pair) * (1.0 / N)
            o_n = dev * lax.rsqrt(head_sum(dev * dev) * (1.0 / N) + GN_EPS) * gnw_ref[:, sl] + gnb_ref[:, sl]
            y_tiles.append(o_n + head_sum(ck["bonus"][:, sl]) * ck["v"][p])
        y_ref[0, ck["rows"]] = jnp.concatenate(y_tiles, axis=1)

    @pl.when(step == pl.num_programs(1) - 1)
    def _():
        for p in pairs:
            s_ref[0, per_tile * p] = sbd_ref[p][:N, :N]
            s_ref[0, per_tile * p + 1] = sbd_ref[p][N:, N:]


def _rwkv_scan(r, lw, k, v, kk, b, s0, p, tri):
    bsz, t, d = r.shape
    n_sub = min(SCAN_CHUNKS_PER_STEP, pl.cdiv(t, CHUNK))
    seq = pl.BlockSpec((1, n_sub * CHUNK, d), lambda bi, c: (bi, c, 0))
    st = pl.BlockSpec((1, N_HEADS, HEAD_DIM, HEAD_DIM), lambda bi, c: (bi, 0, 0, 0))
    consts = [p["r_k"], p["gn_w"], p["gn_b"], tri]
    return pl.pallas_call(
        functools.partial(_rwkv_scan_kernel, t, n_sub),
        grid=(bsz, pl.cdiv(t, n_sub * CHUNK)),
        in_specs=[seq] * 6 + [st] + [_resident(a.shape) for a in consts],
        out_specs=[seq, st],
        out_shape=[jax.ShapeDtypeStruct((bsz, t, d), f32), jax.ShapeDtypeStruct(s0.shape, f32)],
        scratch_shapes=[pltpu.VMEM((N_HEADS * HEAD_DIM // LANES, LANES, LANES), f32)],
        compiler_params=_params(("parallel", "arbitrary")),
        name="rwkv_scan",
    )(r, lw, k, v, kk, b, s0, *consts)


SB_PRE_HEADS = 4


def _sb_pre_kernel(reuse_h, x_ref, ng_ref, wq_ref, wkv_ref, qg_ref, kg_ref, eseg_ref, eexp_ref, q_out, kt_out, vt_out,
                   h_ref):
    hg, n = SB_PRE_HEADS, HEAD_DIM
    t = x_ref.shape[1]

    def normed():
        return _rms(x_ref[0], ng_ref[...]).astype(bf16)

    if reuse_h:
        @pl.when(pl.program_id(1) == 0)
        def _():
            h_ref[...] = normed()
        h = h_ref[...]
    else:
        h = normed()
    q = jnp.dot(h, wq_ref[...], preferred_element_type=f32)
    ms = _head_sum(q * q, eseg_ref, eexp_ref) * (1.0 / n)
    q_out[0] = q * _head_expand(lax.rsqrt(ms + RMS_EPS), eexp_ref) * qg_ref[...]

    kv = lax.dot_general(wkv_ref[0], h, _NT, preferred_element_type=f32)
    k = kv[:hg * n].reshape(hg, n, t)
    ms = jnp.mean(k * k, axis=1, keepdims=True)
    kt_out[0] = k * lax.rsqrt(ms + RMS_EPS) * kg_ref[...]
    vt_out[0] = kv[hg * n:].reshape(hg, n, t)


def _sb_pre(x, ng, wq, wkv_t, qg, kg_col, eseg, eexp, streams_inner):
    b, t, d = x.shape
    hg, n = SB_PRE_HEADS, HEAD_DIM
    w = hg * n
    ins = [x, ng, wq, wkv_t, qg, kg_col, eseg[:w], eexp[:, :w]]
    if streams_inner:
        grid = (N_HEADS // hg, b)
        ix = lambda f: (lambda g, bi: f(bi, g))
    else:
        grid = (b, N_HEADS // hg)
        ix = lambda f: f
    tspec = pl.BlockSpec((1, hg, n, t), ix(lambda bi, g: (bi, g, 0, 0)))
    tout = jax.ShapeDtypeStruct((b, N_HEADS, n, t), f32)
    return pl.pallas_call(
        functools.partial(_sb_pre_kernel, not streams_inner),
        grid=grid,
        in_specs=[
            pl.BlockSpec((1, t, d), ix(lambda bi, g: (bi, 0, 0)),
                         pipeline_mode=None if streams_inner else pl.Buffered(1)),
            _resident(ng.shape),
            pl.BlockSpec((d, w), ix(lambda bi, g: (0, g))),
            pl.BlockSpec((1, 2 * w, d), ix(lambda bi, g: (g, 0, 0))),
            pl.BlockSpec((1, w), ix(lambda bi, g: (0, g))),
            _resident(kg_col.shape),
            _resident((w, LANES)),
            _resident((LANES, w)),
        ],
        out_specs=[pl.BlockSpec((1, t, w), ix(lambda bi, g: (bi, 0, g))), tspec, tspec],
        out_shape=[jax.ShapeDtypeStruct((b, t, d), f32), tout, tout],
        scratch_shapes=[pltpu.VMEM((t, d), bf16)],
        compiler_params=_params(("arbitrary", "arbitrary")),
        name="sb_pre",
    )(*ins)


SB_QBLOCK = 128
SB_WINDOW = 3 * SB_QBLOCK
SB_QBLOCKS_PER_STEP = 8
SB_PAST_WINDOW = 256
SB_SAMPLE_HEADS = 8


def _later_matrix(n):
    i = jnp.arange(n)
    return (i[:, None] > i[None, :]).astype(bf16)


def _sb_windows(qs, kts, vts, tri, carries, masks):
    n = range(len(qs))
    z = [_mm(qs[i], kts[i]) * (HEAD_DIM ** -0.5) for i in n]
    log_beta = [jnp.minimum(x, 0.0) - jnp.log(1.0 + jnp.exp(-jnp.abs(x))) for x in z]
    log_1m = [log_beta[i] - z[i] for i in n]
    log_1m = [x if m is None else jnp.where(m, x, 0.0) for x, m in zip(log_1m, masks)]
    later = [_mm(x, tri) for x in log_1m]
    a = [jnp.exp(log_beta[i] + later[i] + carries[i]) for i in n]
    a = [x if m is None else jnp.where(m, x, 0.0) for x, m in zip(a, masks)]
    outs = [_mm(a[i], vts[i], _NT) for i in n]
    return outs, [carries[i] + later[i][:, :1] + log_1m[i][:, :1] for i in n]


def _any_alive(kends, carries):
    flags = [jnp.logical_and(k > 0, jnp.max(c) > DEAD_LOG) for k, c in zip(kends, carries)]
    return functools.reduce(jnp.logical_or, flags)


def _sb_prompt_kernel(q_ref, kt_ref, vt_ref, tri_ref, o_ref):
    step = pl.program_id(2)
    t = q_ref.shape[1]
    N, Q, W, G = HEAD_DIM, SB_QBLOCK, SB_WINDOW, SB_QBLOCKS_PER_STEP
    heads = range(LANES // N)
    tri = tri_ref[...]

    def window(ws, hd):
        return kt_ref[0, hd, :, pl.ds(ws, W)], vt_ref[0, hd, :, pl.ds(ws, W)]

    def earlier_windows(qs, chain_heads, kends, accs, carries):
        col = lax.broadcasted_iota(jnp.int32, (qs[0].shape[0], W), 1)

        def body(st):
            kends, accs, carries = st
            ws = [pl.multiple_of(jnp.maximum(k - W, 0), Q) for k in kends]
            kts, vts = zip(*[window(s, hd) for s, hd in zip(ws, chain_heads)])
            masks = [s + col < k for s, k in zip(ws, kends)]
            parts, carries = _sb_windows(qs, kts, vts, tri, carries, masks)
            return ws, [a + p for a, p in zip(accs, parts)], carries

        return lax.while_loop(lambda st: _any_alive(st[0], st[2]), body, (kends, accs, carries))[1]

    row = lax.broadcasted_iota(jnp.int32, (Q, W), 0)
    col = lax.broadcasted_iota(jnp.int32, (Q, W), 1)
    q0 = [pl.multiple_of((step * G + j) * Q, Q) for j in range(G)]
    ws = [pl.multiple_of(jnp.maximum(x - (W - Q), 0), Q) for x in q0]
    chains = [(j, hd) for j in range(G) for hd in heads]
    chain_heads = [hd for _, hd in chains]
    qs = [q_ref[0, pl.ds(q0[j], Q), hd * N:(hd + 1) * N] for j, hd in chains]
    kts, vts = zip(*[window(ws[j], hd) for j, hd in chains])
    masks = [ws[j] + col < q0[j] + row for j, _ in chains]
    zero = jnp.zeros((Q, 1), f32)
    accs, carries = _sb_windows(qs, kts, vts, tri, [zero] * len(chains), masks)
    accs = earlier_windows(qs, chain_heads, [ws[j] for j, _ in chains], accs, carries)
    for j in range(G):
        o_ref[0, pl.ds(q0[j], Q), :] = jnp.concatenate([accs[j * len(heads) + hd] for hd in heads], axis=1)

    q_full = (t // Q) * Q
    rem = t - q_full
    if rem:
        @pl.when(step == pl.num_programs(2) - 1)
        def _():
            qs = [q_ref[0, q_full:, hd * N:(hd + 1) * N] for hd in heads]
            kts = [kt_ref[0, hd, :, q_full:] for hd in heads]
            vts = [vt_ref[0, hd, :, q_full:] for hd in heads]
            causal = (lax.broadcasted_iota(jnp.int32, (rem, rem), 1)
                      < lax.broadcasted_iota(jnp.int32, (rem, rem), 0))
            zero = jnp.zeros((rem, 1), f32)
            accs, carries = _sb_windows(qs, kts, vts, tri[:rem, :rem], [zero] * len(heads), [causal] * len(heads))
            accs = earlier_windows(qs, list(heads), [jnp.int32(q_full)] * len(heads), accs, carries)
            o_ref[0, q_full:, :] = jnp.concatenate(accs, axis=1)


def _sb_prompt(q, kt, vt):
    b, t, d = q.shape
    n_full = t // SB_QBLOCK
    assert n_full % SB_QBLOCKS_PER_STEP == 0 and t >= SB_WINDOW
    tri = _later_matrix(SB_WINDOW)
    qspec = pl.BlockSpec((1, t, LANES), lambda bi, hp, s: (bi, 0, hp))
    kvspec = pl.BlockSpec((1, LANES // HEAD_DIM, HEAD_DIM, t), lambda bi, hp, s: (bi, hp, 0, 0))
    return pl.pallas_call(
        _sb_prompt_kernel,
        grid=(b, d // LANES, n_full // SB_QBLOCKS_PER_STEP),
        in_specs=[qspec, kvspec, kvspec, _resident(tri.shape)],
        out_specs=qspec,
        out_shape=jax.ShapeDtypeStruct((b, t, d), f32),
        compiler_params=_params(("parallel", "parallel", "arbitrary")),
        name="sb_prompt",
    )(q, kt, vt, tri)


def _sb_sample_kernel(q_ref, ktn_ref, vtn_ref, ktp_ref, vtp_ref, kt_hbm, vt_hbm, tri_ref, o_ref, kbuf, vbuf, sem):
    bi, g = pl.program_id(0), pl.program_id(1)
    N = HEAD_DIM
    hg, t, wp = ktn_ref.shape[1], q_ref.shape[1], ktp_ref.shape[3]
    heads = range(hg)
    tri = tri_ref[...]

    qs = [q_ref[0, :, hd * N:(hd + 1) * N] for hd in heads]
    kts = [jnp.concatenate([ktp_ref[0, hd], ktn_ref[0, hd]], axis=1) for hd in heads]
    vts = [jnp.concatenate([vtp_ref[0, hd], vtn_ref[0, hd]], axis=1) for hd in heads]
    row = lax.broadcasted_iota(jnp.int32, (t, wp + t), 0)
    col = lax.broadcasted_iota(jnp.int32, (t, wp + t), 1)
    mask = col - wp < row
    accs, carries = _sb_windows(qs, kts, vts, tri, [jnp.zeros((t, 1), f32)] * hg, [mask] * hg)

    def fetch(src, dst, ws, slot):
        return pltpu.make_async_copy(src.at[bi, pl.ds(g * hg, hg), :, pl.ds(ws, wp)], dst, sem.at[slot])

    def body(st):
        kend, accs, carries = st
        ws = pl.multiple_of(kend - wp, wp)
        copies = [fetch(kt_hbm, kbuf, ws, 0), fetch(vt_hbm, vbuf, ws, 1)]
        for c in copies:
            c.start()
        for c in copies:
            c.wait()
        parts, carries = _sb_windows(qs, [kbuf[hd] for hd in heads], [vbuf[hd] for hd in heads],
                                     tri[:wp, :wp], carries, [None] * hg)
        return ws, [a + p for a, p in zip(accs, parts)], carries

    kend = jnp.int32(kt_hbm.shape[3] - wp)
    accs = lax.while_loop(lambda st: _any_alive([st[0]] * hg, st[2]), body, (kend, accs, carries))[1]
    o_ref[0] = jnp.concatenate(accs, axis=1)


def _sb_sample(q, kt, vt, kt_past, vt_past):
    b, t, d = q.shape
    hg, n, wp = SB_SAMPLE_HEADS, HEAD_DIM, SB_PAST_WINDOW
    p = kt_past.shape[3]
    assert p % wp == 0
    tri = _later_matrix(wp + t)
    qspec = pl.BlockSpec((1, t, hg * n), lambda bi, g: (bi, 0, g))
    new = pl.BlockSpec((1, hg, n, t), lambda bi, g: (bi, g, 0, 0))
    last = pl.BlockSpec((1, hg, n, wp), lambda bi, g: (bi, g, 0, p // wp - 1))
    hbm = pl.BlockSpec(memory_space=pl.ANY)
    return pl.pallas_call(
        _sb_sample_kernel,
        grid=(b, N_HEADS // hg),
        in_specs=[qspec, new, new, last, last, hbm, hbm, _resident(tri.shape)],
        out_specs=qspec,
        out_shape=jax.ShapeDtypeStruct((b, t, d), f32),
        scratch_shapes=[pltpu.VMEM((hg, n, wp), f32), pltpu.VMEM((hg, n, wp), f32), pltpu.SemaphoreType.DMA((2,))],
        compiler_params=_params(("parallel", "parallel")),
        name="sb_sample",
    )(q, kt, vt, kt_past, vt_past, kt_past, vt_past, tri)


def _grouped_kv_t(w_kv):
    d = w_kv.shape[0]
    gw = SB_PRE_HEADS * HEAD_DIM
    k_t = w_kv[:, :d].T.reshape(d // gw, gw, d)
    v_t = w_kv[:, d:].T.reshape(d // gw, gw, d)
    return jnp.concatenate([k_t, v_t], axis=1)


def _stack(x, s0, shift0, kt_past, vt_past, w, tm, tt, head=None, tm_last=None):
    b, t0, d = x.shape
    skip = 0 if head is None else head.shape[0]
    t = t0 + skip
    m = b * t
    flat = lambda a: a.reshape(1, m, d)

    def ffn(x2, i, j, mix=None):
        if mix is not None:
            mix = (flat(mix[0]), None if mix[1] is None else flat(mix[1]), mix[2])
        return _ffn(flat(x2), w["ffn_g"][i][j], w["ffn_wg"], w["ffn_wu"], w["ffn_wo"], (i, j),
                    tm, mix).reshape(m, d)

    if head is None:
        x2 = ffn(x.reshape(m, d), 0, 0)
    else:
        x2 = _ffn(x, w["ffn_g"][0][0], w["ffn_wg"], w["ffn_wu"], w["ffn_wo"], (0, 0), tm,
                  head=head).reshape(m, d)
    rw = w["rwkv"]
    r, lw, k, v, kk, bb, g, shift = _rwkv_pre(x2.reshape(b, t, d), shift0.reshape(b, 1, d), w["mix_g"][0], rw,
                                              w["eseg"], w["eexp"], tt)
    y, s_fin = _rwkv_scan(r, lw, k, v, kk, bb, s0, rw, w["tri_chunk"])
    x2 = ffn(x2, 0, 1, (y, g, rw["w_o"]))

    x2 = ffn(x2, 1, 0)
    sb = w["sb"]
    q, kt, vt = _sb_pre(x2.reshape(b, t, d), w["mix_g"][1], sb["w_q"], sb["w_kv_t"], sb["q_g"], sb["k_g_col"],
                        w["eseg"], w["eexp"], streams_inner=4 * t < 6 * SB_PRE_HEADS * HEAD_DIM)
    if kt_past is None:
        o = _sb_prompt(q, kt, vt)
    else:
        o = _sb_sample(q, kt, vt, kt_past, vt_past)
    if skip:
        y = _ffn(x2.reshape(b, t, d), w["ffn_g"][1][1], w["ffn_wg"], w["ffn_wu"], w["ffn_wo"], (1, 1),
                 tm_last, (o, None, sb["w_o"]), skip)
    else:
        y = ffn(x2, 1, 1, (o, None, sb["w_o"])).reshape(b, t, d)
    return y, s_fin, shift.reshape(b, d), kt, vt


def kernel(x_prompt, x_sample, state_rwkv_wkv, state_rwkv_shift, cache_sb_k, cache_sb_v, meta_tokens,
           ffn_norm_g, ffn_w_in, ffn_w_out, mix_norm_g, rwkv_mu, rwkv_w_rkv, rwkv_w0, rwkv_w1, rwkv_w2,
           rwkv_a0, rwkv_a1, rwkv_a2, rwkv_g1, rwkv_g2, rwkv_k_k, rwkv_k_a, rwkv_r_k, rwkv_gn_w, rwkv_gn_b,
           rwkv_w_o, sb_w_qkv, sb_q_norm_g, sb_k_norm_g, sb_w_o):
    d = D_MODEL
    depth = ffn_w_in.shape[0]
    d_ff = ffn_w_out.shape[2]
    pad = (-d_ff) % FF_CHUNK
    w_gate = jnp.pad(ffn_w_in[..., :d_ff], ((0, 0), (0, 0), (0, 0), (0, pad))).astype(bf16)
    w_up = jnp.pad(ffn_w_in[..., d_ff:], ((0, 0), (0, 0), (0, 0), (0, pad))).astype(bf16)
    w_out = jnp.pad(ffn_w_out, ((0, 0), (0, 0), (0, pad), (0, 0))).astype(bf16)
    vec = lambda a: a.reshape(1, d).astype(f32)
    head_of_lane = jnp.arange(d) // HEAD_DIM
    eseg = (head_of_lane[:, None] == jnp.arange(LANES)[None, :]).astype(bf16)
    w = {
        "ffn_g": [[vec(ffn_norm_g[i, j]) for j in range(2)] for i in range(depth)],
        "ffn_wg": w_gate, "ffn_wu": w_up, "ffn_wo": w_out,
        "mix_g": [vec(mix_norm_g[i]) for i in range(depth)],
        "eseg": eseg, "eexp": eseg.T,
        "tri_chunk": (jnp.arange(CHUNK)[:, None] >= jnp.arange(CHUNK)[None, :]).astype(bf16),
        "rwkv": {
            "mu": rwkv_mu, "w_rkv": rwkv_w_rkv.astype(bf16), "w0": vec(rwkv_w0), "w1": rwkv_w1.astype(bf16),
            "w2": rwkv_w2.astype(bf16), "a0": vec(rwkv_a0), "a1": rwkv_a1.astype(bf16), "a2": rwkv_a2.astype(bf16),
            "g1": rwkv_g1.astype(bf16), "g2": rwkv_g2.astype(bf16), "k_k": vec(rwkv_k_k), "k_a": vec(rwkv_k_a),
            "r_k": vec(rwkv_r_k), "gn_w": vec(rwkv_gn_w), "gn_b": vec(rwkv_gn_b), "w_o": rwkv_w_o.astype(bf16),
        },
        "sb": {
            "w_q": sb_w_qkv[:, :d].astype(bf16), "w_kv_t": _grouped_kv_t(sb_w_qkv[:, d:]).astype(bf16),
            "q_g": vec(jnp.tile(sb_q_norm_g, N_HEADS)), "k_g_col": sb_k_norm_g.reshape(HEAD_DIM, 1).astype(f32),
            "w_o": sb_w_o.astype(bf16),
        },
    }

    bp = x_prompt.shape[0]
    s0_p = jnp.zeros((bp, N_HEADS, HEAD_DIM, HEAD_DIM), f32)
    shift0_p = jnp.zeros((bp, d), f32)
    y_prompt, s_p, shift_p, kt_p, vt_p = _stack(x_prompt, s0_p, shift0_p, None, None, w, tm=688, tt=344,
                                                head=meta_tokens.astype(f32), tm_last=512)
    frames_last = lambda a: jnp.swapaxes(a, 2, 3)
    ys, s_s, shift_s, kt_s, vt_s = _stack(x_sample, state_rwkv_wkv, state_rwkv_shift, frames_last(cache_sb_k),
                                          frames_last(cache_sb_v), w, tm=512, tt=64)
    return (y_prompt, ys, s_p, shift_p, frames_last(kt_p), frames_last(vt_p), s_s, shift_s,
            frames_last(kt_s), frames_last(vt_s))
```

```python
import functools

import jax
import jax.numpy as jnp
from jax import lax
from jax.experimental import pallas as pl
from jax.experimental.pallas import tpu as pltpu

D_MODEL = 1024
HEAD_DIM = 64
N_HEADS = D_MODEL // HEAD_DIM
RMS_EPS = 1e-6
GN_EPS = 64e-5

LANES = 128
CHUNK = 64
SCAN_CHUNKS_PER_STEP = 3
FF_CHUNK = 256
DEAD_LOG = -104.0
VMEM_LIMIT = 52 * 1024 * 1024

f32 = jnp.float32
bf16 = jnp.bfloat16

_NN = (((1,), (0,)), ((), ()))
_NT = (((1,), (1,)), ((), ()))
_TN = (((0,), (0,)), ((), ()))


def _mm(a, b, dims=_NN):
    return lax.dot_general(a.astype(bf16), b.astype(bf16), dims, preferred_element_type=f32)


def _split(a):
    hi = a.astype(bf16)
    lo = (a - hi.astype(f32)).astype(bf16)
    return hi, lo


def _mm3(a, b, dims=_NN):
    ah, al = _split(a)
    bh, bl = _split(b)
    dot = functools.partial(lax.dot_general, dimension_numbers=dims, preferred_element_type=f32)
    return dot(ah, bh) + (dot(ah, bl) + dot(al, bh))


def _mm_exact_rhs(a, b_bf16, dims=_NN):
    ah, al = _split(a)
    dot = functools.partial(lax.dot_general, dimension_numbers=dims, preferred_element_type=f32)
    return dot(ah, b_bf16) + dot(al, b_bf16)


def _mm_exact_lhs(a_bf16, b, dims=_NN):
    bh, bl = _split(b)
    dot = functools.partial(lax.dot_general, dimension_numbers=dims, preferred_element_type=f32)
    return dot(a_bf16, bh) + dot(a_bf16, bl)


def _rms(x, g):
    return x * lax.rsqrt(jnp.mean(x * x, axis=-1, keepdims=True) + RMS_EPS) * g


def _softplus(y):
    return jnp.maximum(y, 0.0) + jnp.log1p(jnp.exp(-jnp.abs(y)))


def _sigmoid(y):
    return 1.0 / (1.0 + jnp.exp(-y))


def _head_sum(x, eseg_ref, eexp_ref):
    return _mm_exact_rhs(x, eseg_ref[...])


def _head_expand(y, eexp_ref):
    return _mm_exact_rhs(y, eexp_ref[...])


def _full(shape):
    return pl.BlockSpec(shape, lambda *_: (0,) * len(shape))


def _resident(shape):
    return pl.BlockSpec(shape, lambda *_: (0,) * len(shape), pipeline_mode=pl.Buffered(1))


def _params(sem):
    return pltpu.CompilerParams(dimension_semantics=sem, vmem_limit_bytes=VMEM_LIMIT)


def _ffn_kernel(n_mix, n_head, *refs):
    mix, refs = refs[:n_mix], refs[n_mix:]
    head, (x_ref, g_ref, wg_ref, wu_ref, wo_ref, o_ref, acc_ref) = refs[:bool(n_head)], refs[bool(n_head):]
    rows = lambda ref: ref[...].reshape(ref.shape[-2:])
    x = rows(x_ref)
    if n_head:
        moved = jnp.concatenate([head[0][...], x[:x.shape[0] - n_head]], axis=0)
        x = jnp.where(pl.program_id(1) == 0, moved, x)
    if n_mix:
        y = rows(mix[0])
        if n_mix == 3:
            y = y * rows(mix[1])
        x = x + jnp.dot(y.astype(bf16), mix[-1][...], preferred_element_type=f32)
    h = _rms(x, g_ref[...]).astype(bf16)
    n_chunks = wg_ref.shape[1] // FF_CHUNK
    for c in range(n_chunks):
        sl = slice(c * FF_CHUNK, (c + 1) * FF_CHUNK)
        gate = jnp.dot(h, wg_ref[:, sl], preferred_element_type=f32)
        up = jnp.dot(h, wu_ref[:, sl], preferred_element_type=f32)
        act = (gate * _sigmoid(gate) * up).astype(bf16)
        part = jnp.dot(act, wo_ref[sl, :], preferred_element_type=f32)
        if c == 0:
            acc_ref[...] = part
        else:
            acc_ref[...] += part
    o_ref[...] = x + 0.5 * acc_ref[...]


def _ffn(x, g, wg, wu, wo, ij, tm, mix=None, skip=0, head=None):
    b, t, d = x.shape
    fp = wg.shape[-1]
    picked = lambda *shape: pl.BlockSpec((None, None) + shape, lambda *_: (*ij, 0, 0), pipeline_mode=pl.Buffered(1))
    n_head = 0 if head is None else head.shape[0]
    assert not (n_head and (skip or mix is not None))
    t_out = t - skip + n_head
    assert t_out % tm == 0 and skip % 8 == 0 and n_head % 8 == 0 and tm % 8 == 0
    if skip or n_head:
        row_in = pl.BlockSpec(
            (pl.Element(1), pl.Element(tm), pl.Element(d)),
            lambda bi, i: (bi, pl.multiple_of(jnp.maximum(skip - n_head + i * tm, 0), 8), 0))
    else:
        row_in = pl.BlockSpec((None, tm, d), lambda bi, i: (bi, i, 0))
    row_out = pl.BlockSpec((None, tm, d), lambda bi, i: (bi, i, 0))
    mix_ins, mix_specs = [], []
    if mix is not None:
        y, gate, w_proj = mix
        mix_ins = [y] + ([gate] if gate is not None else []) + [w_proj]
        mix_specs = [row_in] * (len(mix_ins) - 1) + [_resident(w_proj.shape)]
    head_ins, head_specs = ([head], [_resident(head.shape)]) if n_head else ([], [])
    return pl.pallas_call(
        functools.partial(_ffn_kernel, len(mix_ins), n_head),
        grid=(b, t_out // tm),
        in_specs=(mix_specs + head_specs
                  + [row_in, _full((1, d)), picked(d, fp), picked(d, fp), picked(fp, d)]),
        out_specs=row_out,
        out_shape=jax.ShapeDtypeStruct((b, t_out, d), f32),
        scratch_shapes=[pltpu.VMEM((tm, d), f32)],
        compiler_params=_params(("parallel", "parallel")),
        name="ffn",
    )(*mix_ins, *head_ins, x, g, wg, wu, wo)


def _rwkv_pre_kernel(x_ref, shift_ref, ng_ref, mu_ref, wrkv_ref, w0_ref, w1_ref, w2_ref,
                     a0_ref, a1_ref, a2_ref, g1_ref, g2_ref, kk_ref, ka_ref, eseg_ref, eexp_ref,
                     r_out, lw_out, k_out, v_out, kk_out, b_out, g_out, shift_out, prev_ref):
    i = pl.program_id(1)
    bs, tt, d = x_ref.shape

    @pl.when(i == 0)
    def _():
        prev_ref[...] = shift_ref[...]

    h = _rms(x_ref[...].reshape(bs * tt, d), ng_ref[...])
    first_row = lax.broadcasted_iota(jnp.int32, (bs, tt, d), 1).reshape(bs * tt, d) == 0
    carried = jnp.broadcast_to(prev_ref[...], (bs, tt, d)).reshape(bs * tt, d)
    h_prev = jnp.where(first_row, carried, pltpu.roll(h, 1, axis=0))
    last = h.reshape(bs, tt, d)[:, tt - 1:tt, :]
    prev_ref[...] = last
    shift_out[...] = last

    dx = h_prev - h
    xr, xw, xk, xv, xa, xg = ((h + dx * mu_ref[j:j + 1, :]).astype(bf16) for j in range(6))
    r = jnp.dot(xr, wrkv_ref[0], preferred_element_type=f32)
    k = jnp.dot(xk, wrkv_ref[1], preferred_element_type=f32)
    v = jnp.dot(xv, wrkv_ref[2], preferred_element_type=f32)
    lora_w = _mm(jnp.tanh(jnp.dot(xw, w1_ref[...], preferred_element_type=f32)), w2_ref[...])
    log_rate = -_softplus(-(w0_ref[...] + lora_w)) - 0.5
    a = _sigmoid(a0_ref[...] + _mm(jnp.dot(xa, a1_ref[...], preferred_element_type=f32), a2_ref[...]))
    g = _mm(_sigmoid(jnp.dot(xg, g1_ref[...], preferred_element_type=f32)), g2_ref[...])

    kk = k * kk_ref[...]
    norm = jnp.sqrt(_head_sum(kk * kk, eseg_ref, eexp_ref))
    kk = kk * _head_expand(1.0 / jnp.maximum(norm, 1e-12), eexp_ref)

    outs = (r, -jnp.exp(log_rate), k * (1.0 + (a - 1.0) * ka_ref[...]), v, kk, kk * a, g)
    for ref, val in zip((r_out, lw_out, k_out, v_out, kk_out, b_out, g_out), outs):
        ref[...] = val.reshape(bs, tt, d)


def _rwkv_pre(x, shift0, ng, p, eseg, eexp, tt, bs):
    b, t, d = x.shape
    seq = pl.BlockSpec((bs, tt, d), lambda bi, i: (bi, i, 0))
    vec = pl.BlockSpec((bs, 1, d), lambda bi, i: (bi, 0, 0))
    out = jax.ShapeDtypeStruct((b, t, d), f32)
    ins = [x, shift0, ng, p["mu"], p["w_rkv"], p["w0"], p["w1"], p["w2"], p["a0"], p["a1"], p["a2"],
           p["g1"], p["g2"], p["k_k"], p["k_a"], eseg, eexp]
    specs = [seq, vec] + [_resident(a.shape) for a in ins[2:]]
    return pl.pallas_call(
        _rwkv_pre_kernel,
        grid=(b // bs, t // tt),
        in_specs=specs,
        out_specs=[seq] * 7 + [vec],
        out_shape=[out] * 7 + [jax.ShapeDtypeStruct((b, 1, d), f32)],
        scratch_shapes=[pltpu.VMEM((bs, 1, d), f32)],
        compiler_params=_params(("parallel", "arbitrary")),
        name="rwkv_pre",
    )(*ins)


def _rwkv_scan_kernel(t_total, n_sub, r_ref, lw_ref, k_ref, v_ref, kk_ref, b_ref, s0_ref, rk_ref, gnw_ref, gnb_ref,
                      tri_ref, y_ref, s_ref, sbd_ref):
    step = pl.program_id(1)
    C, N = CHUNK, HEAD_DIM
    per_tile = LANES // N
    pairs = range(N_HEADS // per_tile)
    tiles = [slice(p * LANES, (p + 1) * LANES) for p in pairs]

    @pl.when(step == 0)
    def _():
        zero = jnp.zeros((N, N), f32)
        for p in pairs:
            sbd_ref[p] = jnp.concatenate(
                [jnp.concatenate([s0_ref[0, per_tile * p], zero], axis=1),
                 jnp.concatenate([zero, s0_ref[0, per_tile * p + 1]], axis=1)], axis=0)

    row = lax.broadcasted_iota(jnp.int32, (C, 2 * C), 0)
    lane = lax.broadcasted_iota(jnp.int32, (C, 2 * C), 1)
    left = lane < C
    col = jnp.where(left, lane, lane - C)
    strict = row > col
    incl = row >= col
    eye_right = jnp.logical_and(row == col, jnp.logical_not(left)).astype(f32)
    head_lanes = [(lax.broadcasted_iota(jnp.int32, (2 * C, LANES), 1) // N) == s for s in range(per_tile)]
    first = lax.broadcasted_iota(jnp.int32, (C, LANES), 1) < N

    chunks = []
    for j in range(n_sub):
        rows = slice(j * C, (j + 1) * C)
        r, lw, k, v, kk, b = (ref[0, rows] for ref in (r_ref, lw_ref, k_ref, v_ref, kk_ref, b_ref))
        if t_total % (n_sub * C):
            valid = (step * n_sub + j) * C + lax.broadcasted_iota(jnp.int32, r.shape, 0) < t_total
            r, lw, k, v, kk, b = (jnp.where(valid, a, 0.0) for a in (r, lw, k, v, kk, b))
        cum = _mm_exact_lhs(tri_ref[...], lw)
        p_t = jnp.exp(cum)
        p_inv = jnp.exp(-cum)
        r_t = r * p_t
        a_t = -kk * jnp.exp(cum - lw)
        b_t = b * p_inv
        k_t = k * p_inv
        gamma = p_t[C - 1:C, :]
        b_g = b_t * gamma
        k_g = k_t * gamma
        chunks.append(dict(
            rows=rows, gamma=gamma, bonus=r * k * rk_ref[...],
            ar=[jnp.concatenate([a_t[:, sl], r_t[:, sl]], axis=0) for sl in tiles],
            bk=[jnp.concatenate([b_t[:, sl], k_t[:, sl]], axis=0) for sl in tiles],
            bkg=[jnp.concatenate([b_g[:, sl], k_g[:, sl]], axis=0) for sl in tiles],
            v=[v[:, sl] for sl in tiles]))

    chains = [(j, p, s) for j in range(n_sub) for p in pairs for s in range(per_tile)]
    ar = [jnp.where(head_lanes[s], chunks[j]["ar"][p], 0.0) for j, p, s in chains]
    x2 = [_mm(ar[i], chunks[j]["bk"][p], _NT) for i, (j, p, _) in enumerate(chains)]
    z = [jnp.where(jnp.logical_and(strict, left), x[:C], 0.0) + eye_right for x in x2]
    for _ in range(6):
        sq = [_mm(x[:, :C], x) for x in z]
        z = [jnp.where(left, q, x + q) for x, q in zip(z, sq)]
    t_inv = [jnp.where(left, 0.0, x) for x in z]
    a_ak = [jnp.where(jnp.logical_and(strict, jnp.logical_not(left)), x[:C], 0.0) for x in x2]
    akv = [_mm(a_ak[i], jnp.concatenate([chunks[j]["v"][p]] * 2, axis=0)) for i, (j, p, _) in enumerate(chains)]
    a_r = [jnp.where(incl, x[C:], 0.0) for x in x2]

    def head_sum(x):
        s0 = jnp.sum(jnp.where(first, x, 0.0), axis=-1, keepdims=True)
        s1 = jnp.sum(jnp.where(first, 0.0, x), axis=-1, keepdims=True)
        return jnp.where(first, s0, s1)

    for j in range(n_sub):
        ck = chunks[j]
        mine = [i for i, (jj, _, _) in enumerate(chains) if jj == j]
        s_old = [sbd_ref[p] for p in pairs]
        xs = {i: _mm_exact_lhs(ar[i].astype(bf16), s_old[chains[i][1]], _NT) for i in mine}
        rhs = {i: xs[i][:C] + akv[i] for i in mine}
        u = {i: _mm(t_inv[i], jnp.concatenate([rhs[i], rhs[i]], axis=0)) for i in mine}
        uv = {i: jnp.concatenate([u[i], ck["v"][chains[i][1]]], axis=0) for i in mine}
        o = {i: xs[i][C:] + _mm(a_r[i], uv[i]) for i in mine}
        s_add = {i: _mm(uv[i], jnp.where(head_lanes[chains[i][2]], ck["bkg"][chains[i][1]], 0.0), _TN) for i in mine}
        y_tiles = []
        for p in pairs:
            i0, sl = mine[per_tile * p], tiles[p]
            sbd_ref[p] = (s_old[p] * ck["gamma"][:, sl]
                          + jnp.concatenate([s_add[i0][:N], s_add[i0 + 1][N:]], axis=0))
            o_pair = jnp.where(first, o[i0], o[i0 + 1])
            dev = o_pair - head_sum(o_pair) * (1.0 / N)
            o_n = dev * lax.rsqrt(head_sum(dev * dev) * (1.0 / N) + GN_EPS) * gnw_ref[:, sl] + gnb_ref[:, sl]
            y_tiles.append(o_n + head_sum(ck["bonus"][:, sl]) * ck["v"][p])
        y_ref[0, ck["rows"]] = jnp.concatenate(y_tiles, axis=1)

    @pl.when(step == pl.num_programs(1) - 1)
    def _():
        for p in pairs:
            s_ref[0, per_tile * p] = sbd_ref[p][:N, :N]
            s_ref[0, per_tile * p + 1] = sbd_ref[p][N:, N:]


def _rwkv_scan(r, lw, k, v, kk, b, s0, p, tri):
    bsz, t, d = r.shape
    n_sub = min(SCAN_CHUNKS_PER_STEP, pl.cdiv(t, CHUNK))
    seq = pl.BlockSpec((1, n_sub * CHUNK, d), lambda bi, c: (bi, c, 0))
    st = pl.BlockSpec((1, N_HEADS, HEAD_DIM, HEAD_DIM), lambda bi, c: (bi, 0, 0, 0))
    consts = [p["r_k"], p["gn_w"], p["gn_b"], tri]
    return pl.pallas_call(
        functools.partial(_rwkv_scan_kernel, t, n_sub),
        grid=(bsz, pl.cdiv(t, n_sub * CHUNK)),
        in_specs=[seq] * 6 + [st] + [_resident(a.shape) for a in consts],
        out_specs=[seq, st],
        out_shape=[jax.ShapeDtypeStruct((bsz, t, d), f32), jax.ShapeDtypeStruct(s0.shape, f32)],
        scratch_shapes=[pltpu.VMEM((N_HEADS * HEAD_DIM // LANES, LANES, LANES), f32)],
        compiler_params=_params(("parallel", "arbitrary")),
        name="rwkv_scan",
    )(r, lw, k, v, kk, b, s0, *consts)


SB_PRE_HEADS = 4


def _sb_pre_kernel(reuse_h, x_ref, ng_ref, wq_ref, wkv_ref, qg_ref, kg_ref, eseg_ref, eexp_ref, q_out, kt_out, vt_out,
                   h_ref):
    hg, n = kt_out.shape[1], HEAD_DIM
    t = x_ref.shape[1]

    def normed():
        return _rms(x_ref[0], ng_ref[...]).astype(bf16)

    if reuse_h:
        @pl.when(pl.program_id(1) == 0)
        def _():
            h_ref[...] = normed()
        h = h_ref[...]
    else:
        h = normed()
    q = jnp.dot(h, wq_ref[...], preferred_element_type=f32)
    ms = _head_sum(q * q, eseg_ref, eexp_ref) * (1.0 / n)
    q_out[0] = q * _head_expand(lax.rsqrt(ms + RMS_EPS), eexp_ref) * qg_ref[...]

    kv = lax.dot_general(wkv_ref[0], h, _NT, preferred_element_type=f32)
    k = kv[:hg * n].reshape(hg, n, t)
    ms = jnp.mean(k * k, axis=1, keepdims=True)
    kt_out[0] = k * lax.rsqrt(ms + RMS_EPS) * kg_ref[...]
    vt_out[0] = kv[hg * n:].reshape(hg, n, t)


def _sb_pre(x, ng, wq, wkv_t, qg, kg_col, eseg, eexp, streams_inner):
    b, t, d = x.shape
    n = HEAD_DIM
    w = wkv_t.shape[1] // 2
    hg = w // n
    ins = [x, ng, wq, wkv_t, qg, kg_col, eseg[:w], eexp[:, :w]]
    if streams_inner:
        grid = (N_HEADS // hg, b)
        ix = lambda f: (lambda g, bi: f(bi, g))
    else:
        grid = (b, N_HEADS // hg)
        ix = lambda f: f
    tspec = pl.BlockSpec((1, hg, n, t), ix(lambda bi, g: (bi, g, 0, 0)))
    tout = jax.ShapeDtypeStruct((b, N_HEADS, n, t), f32)
    return pl.pallas_call(
        functools.partial(_sb_pre_kernel, not streams_inner),
        grid=grid,
        in_specs=[
            pl.BlockSpec((1, t, d), ix(lambda bi, g: (bi, 0, 0)),
                         pipeline_mode=None if streams_inner else pl.Buffered(1)),
            _resident(ng.shape),
            pl.BlockSpec((d, w), ix(lambda bi, g: (0, g))),
            pl.BlockSpec((1, 2 * w, d), ix(lambda bi, g: (g, 0, 0))),
            pl.BlockSpec((1, w), ix(lambda bi, g: (0, g))),
            _resident(kg_col.shape),
            _resident((w, LANES)),
            _resident((LANES, w)),
        ],
        out_specs=[pl.BlockSpec((1, t, w), ix(lambda bi, g: (bi, 0, g))), tspec, tspec],
        out_shape=[jax.ShapeDtypeStruct((b, t, d), f32), tout, tout],
        scratch_shapes=[pltpu.VMEM((t, d), bf16)],
        compiler_params=_params(("arbitrary", "arbitrary")),
        name="sb_pre",
    )(*ins)


SB_QBLOCK = 128
SB_WINDOW = 3 * SB_QBLOCK
SB_QBLOCKS_PER_STEP = 8
SB_PAST_WINDOW = 256
SB_SAMPLE_HEADS = 8


def _later_matrix(n):
    i = jnp.arange(n)
    return (i[:, None] > i[None, :]).astype(bf16)


def _sb_windows(qs, kts, vts, tri, carries, masks):
    n = range(len(qs))
    z = [_mm(qs[i], kts[i]) * (HEAD_DIM ** -0.5) for i in n]
    log_beta = [jnp.minimum(x, 0.0) - jnp.log(1.0 + jnp.exp(-jnp.abs(x))) for x in z]
    log_1m = [log_beta[i] - z[i] for i in n]
    log_1m = [x if m is None else jnp.where(m, x, 0.0) for x, m in zip(log_1m, masks)]
    later = [_mm(x, tri) for x in log_1m]
    a = [jnp.exp(log_beta[i] + later[i] + carries[i]) for i in n]
    a = [x if m is None else jnp.where(m, x, 0.0) for x, m in zip(a, masks)]
    outs = [_mm(a[i], vts[i], _NT) for i in n]
    return outs, [carries[i] + later[i][:, :1] + log_1m[i][:, :1] for i in n]


def _any_alive(kends, carries):
    flags = [jnp.logical_and(k > 0, jnp.max(c) > DEAD_LOG) for k, c in zip(kends, carries)]
    return functools.reduce(jnp.logical_or, flags)


def _sb_prompt_kernel(q_ref, kt_ref, vt_ref, tri_ref, o_ref):
    step = pl.program_id(2)
    t = q_ref.shape[1]
    N, Q, W, G = HEAD_DIM, SB_QBLOCK, SB_WINDOW, SB_QBLOCKS_PER_STEP
    heads = range(LANES // N)
    tri = tri_ref[...]

    def window(ws, hd):
        return kt_ref[0, hd, :, pl.ds(ws, W)], vt_ref[0, hd, :, pl.ds(ws, W)]

    def earlier_windows(qs, chain_heads, kends, accs, carries):
        col = lax.broadcasted_iota(jnp.int32, (qs[0].shape[0], W), 1)

        def body(st):
            kends, accs, carries = st
            ws = [pl.multiple_of(jnp.maximum(k - W, 0), Q) for k in kends]
            kts, vts = zip(*[window(s, hd) for s, hd in zip(ws, chain_heads)])
            masks = [s + col < k for s, k in zip(ws, kends)]
            parts, carries = _sb_windows(qs, kts, vts, tri, carries, masks)
            return ws, [a + p for a, p in zip(accs, parts)], carries

        return lax.while_loop(lambda st: _any_alive(st[0], st[2]), body, (kends, accs, carries))[1]

    row = lax.broadcasted_iota(jnp.int32, (Q, W), 0)
    col = lax.broadcasted_iota(jnp.int32, (Q, W), 1)
    q0 = [pl.multiple_of((step * G + j) * Q, Q) for j in range(G)]
    ws = [pl.multiple_of(jnp.maximum(x - (W - Q), 0), Q) for x in q0]
    chains = [(j, hd) for j in range(G) for hd in heads]
    chain_heads = [hd for _, hd in chains]
    qs = [q_ref[0, pl.ds(q0[j], Q), hd * N:(hd + 1) * N] for j, hd in chains]
    kts, vts = zip(*[window(ws[j], hd) for j, hd in chains])
    masks = [ws[j] + col < q0[j] + row for j, _ in chains]
    zero = jnp.zeros((Q, 1), f32)
    accs, carries = _sb_windows(qs, kts, vts, tri, [zero] * len(chains), masks)
    accs = earlier_windows(qs, chain_heads, [ws[j] for j, _ in chains], accs, carries)
    for j in range(G):
        o_ref[0, pl.ds(q0[j], Q), :] = jnp.concatenate([accs[j * len(heads) + hd] for hd in heads], axis=1)

    q_full = (t // Q) * Q
    rem = t - q_full
    if rem:
        @pl.when(step == pl.num_programs(2) - 1)
        def _():
            qs = [q_ref[0, q_full:, hd * N:(hd + 1) * N] for hd in heads]
            kts = [kt_ref[0, hd, :, q_full:] for hd in heads]
            vts = [vt_ref[0, hd, :, q_full:] for hd in heads]
            causal = (lax.broadcasted_iota(jnp.int32, (rem, rem), 1)
                      < lax.broadcasted_iota(jnp.int32, (rem, rem), 0))
            zero = jnp.zeros((rem, 1), f32)
            accs, carries = _sb_windows(qs, kts, vts, tri[:rem, :rem], [zero] * len(heads), [causal] * len(heads))
            accs = earlier_windows(qs, list(heads), [jnp.int32(q_full)] * len(heads), accs, carries)
            o_ref[0, q_full:, :] = jnp.concatenate(accs, axis=1)


def _sb_prompt(q, kt, vt):
    b, t, d = q.shape
    n_full = t // SB_QBLOCK
    assert n_full % SB_QBLOCKS_PER_STEP == 0 and t >= SB_WINDOW
    tri = _later_matrix(SB_WINDOW)
    qspec = pl.BlockSpec((1, t, LANES), lambda bi, hp, s: (bi, 0, hp))
    kvspec = pl.BlockSpec((1, LANES // HEAD_DIM, HEAD_DIM, t), lambda bi, hp, s: (bi, hp, 0, 0))
    return pl.pallas_call(
        _sb_prompt_kernel,
        grid=(b, d // LANES, n_full // SB_QBLOCKS_PER_STEP),
        in_specs=[qspec, kvspec, kvspec, _resident(tri.shape)],
        out_specs=qspec,
        out_shape=jax.ShapeDtypeStruct((b, t, d), f32),
        compiler_params=_params(("parallel", "parallel", "arbitrary")),
        name="sb_prompt",
    )(q, kt, vt, tri)


def _sb_sample_kernel(q_ref, ktn_ref, vtn_ref, ktp_ref, vtp_ref, kt_hbm, vt_hbm, tri_ref, o_ref, kbuf, vbuf, sem):
    bi, g = pl.program_id(0), pl.program_id(1)
    N = HEAD_DIM
    hg, t, wp = ktn_ref.shape[1], q_ref.shape[1], ktp_ref.shape[3]
    heads = range(hg)
    tri = tri_ref[...]

    qs = [q_ref[0, :, hd * N:(hd + 1) * N] for hd in heads]
    kts = [jnp.concatenate([ktp_ref[0, hd], ktn_ref[0, hd]], axis=1) for hd in heads]
    vts = [jnp.concatenate([vtp_ref[0, hd], vtn_ref[0, hd]], axis=1) for hd in heads]
    row = lax.broadcasted_iota(jnp.int32, (t, wp + t), 0)
    col = lax.broadcasted_iota(jnp.int32, (t, wp + t), 1)
    mask = col - wp < row
    accs, carries = _sb_windows(qs, kts, vts, tri, [jnp.zeros((t, 1), f32)] * hg, [mask] * hg)

    def fetch(src, dst, ws, slot):
        return pltpu.make_async_copy(src.at[bi, pl.ds(g * hg, hg), :, pl.ds(ws, wp)], dst, sem.at[slot])

    def body(st):
        kend, accs, carries = st
        ws = pl.multiple_of(kend - wp, wp)
        copies = [fetch(kt_hbm, kbuf, ws, 0), fetch(vt_hbm, vbuf, ws, 1)]
        for c in copies:
            c.start()
        for c in copies:
            c.wait()
        parts, carries = _sb_windows(qs, [kbuf[hd] for hd in heads], [vbuf[hd] for hd in heads],
                                     tri[:wp, :wp], carries, [None] * hg)
        return ws, [a + p for a, p in zip(accs, parts)], carries

    kend = jnp.int32(kt_hbm.shape[3] - wp)
    accs = lax.while_loop(lambda st: _any_alive([st[0]] * hg, st[2]), body, (kend, accs, carries))[1]
    o_ref[0] = jnp.concatenate(accs, axis=1)


def _sb_sample(q, kt, vt, kt_past, vt_past):
    b, t, d = q.shape
    hg, n, wp = SB_SAMPLE_HEADS, HEAD_DIM, SB_PAST_WINDOW
    p = kt_past.shape[3]
    assert p % wp == 0
    tri = _later_matrix(wp + t)
    qspec = pl.BlockSpec((1, t, hg * n), lambda bi, g: (bi, 0, g))
    new = pl.BlockSpec((1, hg, n, t), lambda bi, g: (bi, g, 0, 0))
    last = pl.BlockSpec((1, hg, n, wp), lambda bi, g: (bi, g, 0, p // wp - 1))
    hbm = pl.BlockSpec(memory_space=pl.ANY)
    return pl.pallas_call(
        _sb_sample_kernel,
        grid=(b, N_HEADS // hg),
        in_specs=[qspec, new, new, last, last, hbm, hbm, _resident(tri.shape)],
        out_specs=qspec,
        out_shape=jax.ShapeDtypeStruct((b, t, d), f32),
        scratch_shapes=[pltpu.VMEM((hg, n, wp), f32), pltpu.VMEM((hg, n, wp), f32), pltpu.SemaphoreType.DMA((2,))],
        compiler_params=_params(("parallel", "parallel")),
        name="sb_sample",
    )(q, kt, vt, kt_past, vt_past, kt_past, vt_past, tri)


def _grouped_kv_t(w_kv, heads_per_group):
    d = w_kv.shape[0]
    gw = heads_per_group * HEAD_DIM
    k_t = w_kv[:, :d].T.reshape(d // gw, gw, d)
    v_t = w_kv[:, d:].T.reshape(d // gw, gw, d)
    return jnp.concatenate([k_t, v_t], axis=1)


def _stack(x, s0, shift0, kt_past, vt_past, w, tm, tt, bs=1, head=None, tm_last=None):
    b, t0, d = x.shape
    skip = 0 if head is None else head.shape[0]
    t = t0 + skip
    m = b * t
    flat = lambda a: a.reshape(1, m, d)

    def ffn(x2, i, j, mix=None):
        if mix is not None:
            mix = (flat(mix[0]), None if mix[1] is None else flat(mix[1]), mix[2])
        return _ffn(flat(x2), w["ffn_g"][i][j], w["ffn_wg"], w["ffn_wu"], w["ffn_wo"], (i, j),
                    tm, mix).reshape(m, d)

    if head is None:
        x2 = ffn(x.reshape(m, d), 0, 0)
    else:
        x2 = _ffn(x, w["ffn_g"][0][0], w["ffn_wg"], w["ffn_wu"], w["ffn_wo"], (0, 0), tm,
                  head=head).reshape(m, d)
    rw = w["rwkv"]
    r, lw, k, v, kk, bb, g, shift = _rwkv_pre(x2.reshape(b, t, d), shift0.reshape(b, 1, d), w["mix_g"][0], rw,
                                              w["eseg"], w["eexp"], tt, bs)
    y, s_fin = _rwkv_scan(r, lw, k, v, kk, bb, s0, rw, w["tri_chunk"])
    x2 = ffn(x2, 0, 1, (y, g, rw["w_o"]))

    x2 = ffn(x2, 1, 0)
    sb = w["sb"]
    short = t <= 2 * LANES
    q, kt, vt = _sb_pre(x2.reshape(b, t, d), w["mix_g"][1], sb["w_q"], sb["w_kv_t_all" if short else "w_kv_t"],
                        sb["q_g"], sb["k_g_col"], w["eseg"], w["eexp"], streams_inner=short)
    if kt_past is None:
        o = _sb_prompt(q, kt, vt)
    else:
        o = _sb_sample(q, kt, vt, kt_past, vt_past)
    if skip:
        y = _ffn(x2.reshape(b, t, d), w["ffn_g"][1][1], w["ffn_wg"], w["ffn_wu"], w["ffn_wo"], (1, 1),
                 tm_last, (o, None, sb["w_o"]), skip)
    else:
        y = ffn(x2, 1, 1, (o, None, sb["w_o"])).reshape(b, t, d)
    return y, s_fin, shift.reshape(b, d), kt, vt


def kernel(x_prompt, x_sample, state_rwkv_wkv, state_rwkv_shift, cache_sb_k, cache_sb_v, meta_tokens,
           ffn_norm_g, ffn_w_in, ffn_w_out, mix_norm_g, rwkv_mu, rwkv_w_rkv, rwkv_w0, rwkv_w1, rwkv_w2,
           rwkv_a0, rwkv_a1, rwkv_a2, rwkv_g1, rwkv_g2, rwkv_k_k, rwkv_k_a, rwkv_r_k, rwkv_gn_w, rwkv_gn_b,
           rwkv_w_o, sb_w_qkv, sb_q_norm_g, sb_k_norm_g, sb_w_o):
    d = D_MODEL
    depth = ffn_w_in.shape[0]
    d_ff = ffn_w_out.shape[2]
    pad = (-d_ff) % FF_CHUNK
    w_gate = jnp.pad(ffn_w_in[..., :d_ff], ((0, 0), (0, 0), (0, 0), (0, pad))).astype(bf16)
    w_up = jnp.pad(ffn_w_in[..., d_ff:].astype(bf16), ((0, 0), (0, 0), (0, 0), (0, pad)))
    w_out = jnp.pad(ffn_w_out.astype(bf16), ((0, 0), (0, 0), (0, pad), (0, 0)))
    vec = lambda a: a.reshape(1, d).astype(f32)
    head_of_lane = jnp.arange(d) // HEAD_DIM
    eseg = (head_of_lane[:, None] == jnp.arange(LANES)[None, :]).astype(bf16)
    w = {
        "ffn_g": [[vec(ffn_norm_g[i, j]) for j in range(2)] for i in range(depth)],
        "ffn_wg": w_gate, "ffn_wu": w_up, "ffn_wo": w_out,
        "mix_g": [vec(mix_norm_g[i]) for i in range(depth)],
        "eseg": eseg, "eexp": eseg.T,
        "tri_chunk": (jnp.arange(CHUNK)[:, None] >= jnp.arange(CHUNK)[None, :]).astype(bf16),
        "rwkv": {
            "mu": rwkv_mu, "w_rkv": rwkv_w_rkv.astype(bf16), "w0": vec(rwkv_w0), "w1": rwkv_w1.astype(bf16),
            "w2": rwkv_w2.astype(bf16), "a0": vec(rwkv_a0), "a1": rwkv_a1.astype(bf16), "a2": rwkv_a2.astype(bf16),
            "g1": rwkv_g1.astype(bf16), "g2": rwkv_g2.astype(bf16), "k_k": vec(rwkv_k_k), "k_a": vec(rwkv_k_a),
            "r_k": vec(rwkv_r_k), "gn_w": vec(rwkv_gn_w), "gn_b": vec(rwkv_gn_b), "w_o": rwkv_w_o.astype(bf16),
        },
        "sb": {
            "w_q": sb_w_qkv[:, :d].astype(bf16), "w_kv_t": _grouped_kv_t(sb_w_qkv[:, d:], SB_PRE_HEADS).astype(bf16),
            "w_kv_t_all": _grouped_kv_t(sb_w_qkv[:, d:], N_HEADS).astype(bf16),
            "q_g": vec(jnp.tile(sb_q_norm_g, N_HEADS)), "k_g_col": sb_k_norm_g.reshape(HEAD_DIM, 1).astype(f32),
            "w_o": sb_w_o.astype(bf16),
        },
    }

    bp = x_prompt.shape[0]
    s0_p = jnp.zeros((bp, N_HEADS, HEAD_DIM, HEAD_DIM), f32)
    shift0_p = jnp.zeros((bp, d), f32)
    y_prompt, s_p, shift_p, kt_p, vt_p = _stack(x_prompt, s0_p, shift0_p, None, None, w, tm=688, tt=344,
                                                head=meta_tokens.astype(f32), tm_last=512)
    frames_last = lambda a: jnp.swapaxes(a, 2, 3)
    ys, s_s, shift_s, kt_s, vt_s = _stack(x_sample, state_rwkv_wkv, state_rwkv_shift, frames_last(cache_sb_k),
                                          frames_last(cache_sb_v), w, tm=512, tt=64, bs=4)
    return (y_prompt, ys, s_p, shift_p, frames_last(kt_p), frames_last(vt_p), s_s, shift_s,
            frames_last(kt_s), frames_last(vt_s))
```

```python
import functools

import jax
import jax.numpy as jnp
from jax import lax
from jax.experimental import pallas as pl
from jax.experimental.pallas import tpu as pltpu

D_MODEL = 1024
HEAD_DIM = 64
N_HEADS = D_MODEL // HEAD_DIM
RMS_EPS = 1e-6
GN_EPS = 64e-5

LANES = 128
CHUNK = 64
SCAN_CHUNKS_PER_STEP = 3
FF_CHUNK = 256
DEAD_LOG = -104.0
VMEM_LIMIT = 52 * 1024 * 1024

f32 = jnp.float32
bf16 = jnp.bfloat16

_NN = (((1,), (0,)), ((), ()))
_NT = (((1,), (1,)), ((), ()))
_TN = (((0,), (0,)), ((), ()))


def _mm(a, b, dims=_NN):
    return lax.dot_general(a.astype(bf16), b.astype(bf16), dims, preferred_element_type=f32)


def _split(a):
    hi = a.astype(bf16)
    lo = (a - hi.astype(f32)).astype(bf16)
    return hi, lo


def _mm3(a, b, dims=_NN):
    ah, al = _split(a)
    bh, bl = _split(b)
    dot = functools.partial(lax.dot_general, dimension_numbers=dims, preferred_element_type=f32)
    return dot(ah, bh) + (dot(ah, bl) + dot(al, bh))


def _mm_exact_rhs(a, b_bf16, dims=_NN):
    ah, al = _split(a)
    dot = functools.partial(lax.dot_general, dimension_numbers=dims, preferred_element_type=f32)
    return dot(ah, b_bf16) + dot(al, b_bf16)


def _mm_exact_lhs(a_bf16, b, dims=_NN):
    bh, bl = _split(b)
    dot = functools.partial(lax.dot_general, dimension_numbers=dims, preferred_element_type=f32)
    return dot(a_bf16, bh) + dot(a_bf16, bl)


def _rms(x, g):
    return x * lax.rsqrt(jnp.mean(x * x, axis=-1, keepdims=True) + RMS_EPS) * g


def _softplus(y):
    return jnp.maximum(y, 0.0) + jnp.log1p(jnp.exp(-jnp.abs(y)))


def _sigmoid(y):
    return 1.0 / (1.0 + jnp.exp(-y))


def _head_sum(x, eseg_ref, eexp_ref):
    return _mm_exact_rhs(x, eseg_ref[...])


def _head_expand(y, eexp_ref):
    return _mm_exact_rhs(y, eexp_ref[...])


def _full(shape):
    return pl.BlockSpec(shape, lambda *_: (0,) * len(shape))


def _resident(shape):
    return pl.BlockSpec(shape, lambda *_: (0,) * len(shape), pipeline_mode=pl.Buffered(1))


def _params(sem):
    return pltpu.CompilerParams(dimension_semantics=sem, vmem_limit_bytes=VMEM_LIMIT)


def _ffn_kernel(n_mix, n_head, *refs):
    mix, refs = refs[:n_mix], refs[n_mix:]
    head, (x_ref, g_ref, wg_ref, wu_ref, wo_ref, o_ref, acc_ref) = refs[:bool(n_head)], refs[bool(n_head):]
    rows = lambda ref: ref[...].reshape(ref.shape[-2:])
    x = rows(x_ref)
    if n_head:
        moved = jnp.concatenate([head[0][...], x[:x.shape[0] - n_head]], axis=0)
        x = jnp.where(pl.program_id(1) == 0, moved, x)
    if n_mix:
        y = rows(mix[0])
        if n_mix == 3:
            y = y * rows(mix[1])
        x = x + jnp.dot(y.astype(bf16), mix[-1][...], preferred_element_type=f32)
    h = _rms(x, g_ref[...]).astype(bf16)
    d_ff = wg_ref.shape[1]
    for c in range(pl.cdiv(d_ff, FF_CHUNK)):
        sl = slice(c * FF_CHUNK, min((c + 1) * FF_CHUNK, d_ff))
        gate = jnp.dot(h, wg_ref[:, sl], preferred_element_type=f32)
        up = jnp.dot(h, wu_ref[:, sl], preferred_element_type=f32)
        act = (gate * _sigmoid(gate) * up).astype(bf16)
        part = jnp.dot(act, wo_ref[sl, :], preferred_element_type=f32)
        if c == 0:
            acc_ref[...] = part
        else:
            acc_ref[...] += part
    o_ref[...] = x + 0.5 * acc_ref[...]


def _ffn(x, g, wg, wu, wo, ij, tm, mix=None, skip=0, head=None):
    b, t, d = x.shape
    fp = wg.shape[-1]
    picked = lambda *shape: pl.BlockSpec((None, None) + shape, lambda *_: (*ij, 0, 0), pipeline_mode=pl.Buffered(1))
    n_head = 0 if head is None else head.shape[0]
    assert not (n_head and (skip or mix is not None))
    t_out = t - skip + n_head
    assert t_out % tm == 0 and skip % 8 == 0 and n_head % 8 == 0 and tm % 8 == 0
    if skip or n_head:
        row_in = pl.BlockSpec(
            (pl.Element(1), pl.Element(tm), pl.Element(d)),
            lambda bi, i: (bi, pl.multiple_of(jnp.maximum(skip - n_head + i * tm, 0), 8), 0))
    else:
        row_in = pl.BlockSpec((None, tm, d), lambda bi, i: (bi, i, 0))
    row_out = pl.BlockSpec((None, tm, d), lambda bi, i: (bi, i, 0))
    mix_ins, mix_specs = [], []
    if mix is not None:
        y, gate, w_proj = mix
        mix_ins = [y] + ([gate] if gate is not None else []) + [w_proj]
        mix_specs = [row_in] * (len(mix_ins) - 1) + [_resident(w_proj.shape)]
    head_ins, head_specs = ([head], [_resident(head.shape)]) if n_head else ([], [])
    return pl.pallas_call(
        functools.partial(_ffn_kernel, len(mix_ins), n_head),
        grid=(b, t_out // tm),
        in_specs=(mix_specs + head_specs
                  + [row_in, _full((1, d)), picked(d, fp), picked(d, fp), picked(fp, d)]),
        out_specs=row_out,
        out_shape=jax.ShapeDtypeStruct((b, t_out, d), f32),
        scratch_shapes=[pltpu.VMEM((tm, d), f32)],
        compiler_params=_params(("parallel", "parallel")),
        name="ffn",
    )(*mix_ins, *head_ins, x, g, wg, wu, wo)


def _rwkv_pre_kernel(x_ref, shift_ref, ng_ref, mu_ref, wrkv_ref, w0_ref, w1_ref, w2_ref,
                     a0_ref, a1_ref, a2_ref, g1_ref, g2_ref, kk_ref, ka_ref, eseg_ref, eexp_ref,
                     r_out, lw_out, k_out, v_out, kk_out, b_out, g_out, shift_out, prev_ref):
    i = pl.program_id(1)
    bs, tt, d = x_ref.shape

    @pl.when(i == 0)
    def _():
        prev_ref[...] = shift_ref[...]

    h = _rms(x_ref[...].reshape(bs * tt, d), ng_ref[...])
    first_row = lax.broadcasted_iota(jnp.int32, (bs, tt, d), 1).reshape(bs * tt, d) == 0
    carried = jnp.broadcast_to(prev_ref[...], (bs, tt, d)).reshape(bs * tt, d)
    h_prev = jnp.where(first_row, carried, pltpu.roll(h, 1, axis=0))
    last = h.reshape(bs, tt, d)[:, tt - 1:tt, :]
    prev_ref[...] = last
    shift_out[...] = last

    dx = h_prev - h
    xr, xw, xk, xv, xa, xg = ((h + dx * mu_ref[j:j + 1, :]).astype(bf16) for j in range(6))
    r = jnp.dot(xr, wrkv_ref[0], preferred_element_type=f32)
    k = jnp.dot(xk, wrkv_ref[1], preferred_element_type=f32)
    v = jnp.dot(xv, wrkv_ref[2], preferred_element_type=f32)
    lora_w = _mm(jnp.tanh(jnp.dot(xw, w1_ref[...], preferred_element_type=f32)), w2_ref[...])
    log_rate = -_softplus(-(w0_ref[...] + lora_w)) - 0.5
    a = _sigmoid(a0_ref[...] + _mm(jnp.dot(xa, a1_ref[...], preferred_element_type=f32), a2_ref[...]))
    g = _mm(_sigmoid(jnp.dot(xg, g1_ref[...], preferred_element_type=f32)), g2_ref[...])

    kk = k * kk_ref[...]
    norm = jnp.sqrt(_head_sum(kk * kk, eseg_ref, eexp_ref))
    kk = kk * _head_expand(1.0 / jnp.maximum(norm, 1e-12), eexp_ref)

    outs = (r, -jnp.exp(log_rate), k * (1.0 + (a - 1.0) * ka_ref[...]), v, kk, kk * a, g)
    for ref, val in zip((r_out, lw_out, k_out, v_out, kk_out, b_out, g_out), outs):
        ref[...] = val.reshape(bs, tt, d)


def _rwkv_pre(x, shift0, ng, p, eseg, eexp, tt, bs):
    b, t, d = x.shape
    seq = pl.BlockSpec((bs, tt, d), lambda bi, i: (bi, i, 0))
    vec = pl.BlockSpec((bs, 1, d), lambda bi, i: (bi, 0, 0))
    out = jax.ShapeDtypeStruct((b, t, d), f32)
    ins = [x, shift0, ng, p["mu"], p["w_rkv"], p["w0"], p["w1"], p["w2"], p["a0"], p["a1"], p["a2"],
           p["g1"], p["g2"], p["k_k"], p["k_a"], eseg, eexp]
    specs = [seq, vec] + [_resident(a.shape) for a in ins[2:]]
    return pl.pallas_call(
        _rwkv_pre_kernel,
        grid=(b // bs, t // tt),
        in_specs=specs,
        out_specs=[seq] * 7 + [vec],
        out_shape=[out] * 7 + [jax.ShapeDtypeStruct((b, 1, d), f32)],
        scratch_shapes=[pltpu.VMEM((bs, 1, d), f32)],
        compiler_params=_params(("parallel", "arbitrary")),
        name="rwkv_pre",
    )(*ins)


def _rwkv_scan_kernel(t_total, n_sub, r_ref, lw_ref, k_ref, v_ref, kk_ref, b_ref, s0_ref, rk_ref, gnw_ref, gnb_ref,
                      tri_ref, y_ref, s_ref, sbd_ref):
    step = pl.program_id(1)
    C, N = CHUNK, HEAD_DIM
    per_tile = LANES // N
    pairs = range(N_HEADS // per_tile)
    tiles = [slice(p * LANES, (p + 1) * LANES) for p in pairs]

    @pl.when(step == 0)
    def _():
        zero = jnp.zeros((N, N), f32)
        for p in pairs:
            sbd_ref[p] = jnp.concatenate(
                [jnp.concatenate([s0_ref[0, per_tile * p], zero], axis=1),
                 jnp.concatenate([zero, s0_ref[0, per_tile * p + 1]], axis=1)], axis=0)

    row = lax.broadcasted_iota(jnp.int32, (C, 2 * C), 0)
    lane = lax.broadcasted_iota(jnp.int32, (C, 2 * C), 1)
    left = lane < C
    col = jnp.where(left, lane, lane - C)
    strict = row > col
    incl = row >= col
    eye_right = jnp.logical_and(row == col, jnp.logical_not(left)).astype(f32)
    head_lanes = [(lax.broadcasted_iota(jnp.int32, (2 * C, LANES), 1) // N) == s for s in range(per_tile)]
    first = lax.broadcasted_iota(jnp.int32, (C, LANES), 1) < N

    chunks = []
    for j in range(n_sub):
        rows = slice(j * C, (j + 1) * C)
        r, lw, k, v, kk, b = (ref[0, rows] for ref in (r_ref, lw_ref, k_ref, v_ref, kk_ref, b_ref))
        if t_total % (n_sub * C):
            valid = (step * n_sub + j) * C + lax.broadcasted_iota(jnp.int32, r.shape, 0) < t_total
            r, lw, k, v, kk, b = (jnp.where(valid, a, 0.0) for a in (r, lw, k, v, kk, b))
        cum = _mm_exact_lhs(tri_ref[...], lw)
        p_t = jnp.exp(cum)
        p_inv = jnp.exp(-cum)
        r_t = r * p_t
        a_t = -kk * jnp.exp(cum - lw)
        b_t = b * p_inv
        k_t = k * p_inv
        gamma = p_t[C - 1:C, :]
        b_g = b_t * gamma
        k_g = k_t * gamma
        chunks.append(dict(
            rows=rows, gamma=gamma, bonus=r * k * rk_ref[...],
            ar=[jnp.concatenate([a_t[:, sl], r_t[:, sl]], axis=0) for sl in tiles],
            bk=[jnp.concatenate([b_t[:, sl], k_t[:, sl]], axis=0) for sl in tiles],
            bkg=[jnp.concatenate([b_g[:, sl], k_g[:, sl]], axis=0) for sl in tiles],
            v=[v[:, sl] for sl in tiles]))

    chains = [(j, p, s) for j in range(n_sub) for p in pairs for s in range(per_tile)]
    ar = [jnp.where(head_lanes[s], chunks[j]["ar"][p], 0.0) for j, p, s in chains]
    x2 = [_mm(ar[i], chunks[j]["bk"][p], _NT) for i, (j, p, _) in enumerate(chains)]
    z = [jnp.where(jnp.logical_and(strict, left), x[:C], 0.0) + eye_right for x in x2]
    for _ in range(6):
        sq = [_mm(x[:, :C], x) for x in z]
        z = [jnp.where(left, q, x + q) for x, q in zip(z, sq)]
    t_inv = [jnp.where(left, 0.0, x) for x in z]
    a_ak = [jnp.where(jnp.logical_and(strict, jnp.logical_not(left)), x[:C], 0.0) for x in x2]
    akv = [_mm(a_ak[i], jnp.concatenate([chunks[j]["v"][p]] * 2, axis=0)) for i, (j, p, _) in enumerate(chains)]
    a_r = [jnp.where(incl, x[C:], 0.0) for x in x2]

    def head_sum(x):
        s0 = jnp.sum(jnp.where(first, x, 0.0), axis=-1, keepdims=True)
        s1 = jnp.sum(jnp.where(first, 0.0, x), axis=-1, keepdims=True)
        return jnp.where(first, s0, s1)

    for j in range(n_sub):
        ck = chunks[j]
        mine = [i for i, (jj, _, _) in enumerate(chains) if jj == j]
        s_old = [sbd_ref[p] for p in pairs]
        xs = {i: _mm_exact_lhs(ar[i].astype(bf16), s_old[chains[i][1]], _NT) for i in mine}
        rhs = {i: xs[i][:C] + akv[i] for i in mine}
        u = {i: _mm(t_inv[i], jnp.concatenate([rhs[i], rhs[i]], axis=0)) for i in mine}
        uv = {i: jnp.concatenate([u[i], ck["v"][chains[i][1]]], axis=0) for i in mine}
        o = {i: xs[i][C:] + _mm(a_r[i], uv[i]) for i in mine}
        s_add = {i: _mm(uv[i], jnp.where(head_lanes[chains[i][2]], ck["bkg"][chains[i][1]], 0.0), _TN) for i in mine}
        y_tiles = []
        for p in pairs:
            i0, sl = mine[per_tile * p], tiles[p]
            sbd_ref[p] = (s_old[p] * ck["gamma"][:, sl]
                          + jnp.concatenate([s_add[i0][:N], s_add[i0 + 1][N:]], axis=0))
            o_pair = jnp.where(first, o[i0], o[i0 + 1])
            dev = o_pair - head_sum(o_pair) * (1.0 / N)
            o_n = dev * lax.rsqrt(head_sum(dev * dev) * (1.0 / N) + GN_EPS) * gnw_ref[:, sl] + gnb_ref[:, sl]
            y_tiles.append(o_n + head_sum(ck["bonus"][:, sl]) * ck["v"][p])
        y_ref[0, ck["rows"]] = jnp.concatenate(y_tiles, axis=1)

    @pl.when(step == pl.num_programs(1) - 1)
    def _():
        for p in pairs:
            s_ref[0, per_tile * p] = sbd_ref[p][:N, :N]
            s_ref[0, per_tile * p + 1] = sbd_ref[p][N:, N:]


def _rwkv_scan(r, lw, k, v, kk, b, s0, p, tri):
    bsz, t, d = r.shape
    n_sub = min(SCAN_CHUNKS_PER_STEP, pl.cdiv(t, CHUNK))
    seq = pl.BlockSpec((1, n_sub * CHUNK, d), lambda bi, c: (bi, c, 0))
    st = pl.BlockSpec((1, N_HEADS, HEAD_DIM, HEAD_DIM), lambda bi, c: (bi, 0, 0, 0))
    consts = [p["r_k"], p["gn_w"], p["gn_b"], tri]
    return pl.pallas_call(
        functools.partial(_rwkv_scan_kernel, t, n_sub),
        grid=(bsz, pl.cdiv(t, n_sub * CHUNK)),
        in_specs=[seq] * 6 + [st] + [_resident(a.shape) for a in consts],
        out_specs=[seq, st],
        out_shape=[jax.ShapeDtypeStruct((bsz, t, d), f32), jax.ShapeDtypeStruct(s0.shape, f32)],
        scratch_shapes=[pltpu.VMEM((N_HEADS * HEAD_DIM // LANES, LANES, LANES), f32)],
        compiler_params=_params(("parallel", "arbitrary")),
        name="rwkv_scan",
    )(r, lw, k, v, kk, b, s0, *consts)


SB_PRE_HEADS = 4


def _sb_pre_kernel(reuse_h, x_ref, ng_ref, wq_ref, wkv_ref, qg_ref, kg_ref, eseg_ref, eexp_ref, q_out, kt_out, vt_out,
                   h_ref):
    hg, n = kt_out.shape[1], HEAD_DIM
    t = x_ref.shape[1]

    def normed():
        return _rms(x_ref[0], ng_ref[...]).astype(bf16)

    if reuse_h:
        @pl.when(pl.program_id(1) == 0)
        def _():
            h_ref[...] = normed()
        h = h_ref[...]
    else:
        h = normed()
    q = jnp.dot(h, wq_ref[...], preferred_element_type=f32)
    ms = _head_sum(q * q, eseg_ref, eexp_ref) * (1.0 / n)
    q_out[0] = q * _head_expand(lax.rsqrt(ms + RMS_EPS), eexp_ref) * qg_ref[...]

    kv = lax.dot_general(wkv_ref[0], h, _NT, preferred_element_type=f32)
    k = kv[:hg * n].reshape(hg, n, t)
    ms = jnp.mean(k * k, axis=1, keepdims=True)
    kt_out[0] = k * lax.rsqrt(ms + RMS_EPS) * kg_ref[...]
    vt_out[0] = kv[hg * n:].reshape(hg, n, t)


def _sb_pre(x, ng, wq, wkv_t, qg, kg_col, eseg, eexp, streams_inner):
    b, t, d = x.shape
    n = HEAD_DIM
    w = wkv_t.shape[1] // 2
    hg = w // n
    ins = [x, ng, wq, wkv_t, qg, kg_col, eseg[:w], eexp[:, :w]]
    if streams_inner:
        grid = (N_HEADS // hg, b)
        ix = lambda f: (lambda g, bi: f(bi, g))
    else:
        grid = (b, N_HEADS // hg)
        ix = lambda f: f
    tspec = pl.BlockSpec((1, hg, n, t), ix(lambda bi, g: (bi, g, 0, 0)))
    tout = jax.ShapeDtypeStruct((b, N_HEADS, n, t), f32)
    return pl.pallas_call(
        functools.partial(_sb_pre_kernel, not streams_inner),
        grid=grid,
        in_specs=[
            pl.BlockSpec((1, t, d), ix(lambda bi, g: (bi, 0, 0)),
                         pipeline_mode=None if streams_inner else pl.Buffered(1)),
            _resident(ng.shape),
            pl.BlockSpec((d, w), ix(lambda bi, g: (0, g))),
            pl.BlockSpec((1, 2 * w, d), ix(lambda bi, g: (g, 0, 0))),
            pl.BlockSpec((1, w), ix(lambda bi, g: (0, g))),
            _resident(kg_col.shape),
            _resident((w, LANES)),
            _resident((LANES, w)),
        ],
        out_specs=[pl.BlockSpec((1, t, w), ix(lambda bi, g: (bi, 0, g))), tspec, tspec],
        out_shape=[jax.ShapeDtypeStruct((b, t, d), f32), tout, tout],
        scratch_shapes=[pltpu.VMEM((t, d), bf16)],
        compiler_params=_params(("arbitrary", "arbitrary")),
        name="sb_pre",
    )(*ins)


SB_QBLOCK = 128
SB_WINDOW = 3 * SB_QBLOCK
SB_QBLOCKS_PER_STEP = 8
SB_PAST_WINDOW = 256
SB_SAMPLE_HEADS = 8


def _later_matrix(n):
    i = jnp.arange(n)
    return (i[:, None] > i[None, :]).astype(bf16)


def _sb_windows(qs, kts, vts, tri, carries, masks):
    n = range(len(qs))
    z = [_mm(qs[i], kts[i]) * (HEAD_DIM ** -0.5) for i in n]
    log_beta = [jnp.minimum(x, 0.0) - jnp.log(1.0 + jnp.exp(-jnp.abs(x))) for x in z]
    log_1m = [log_beta[i] - z[i] for i in n]
    log_1m = [x if m is None else jnp.where(m, x, 0.0) for x, m in zip(log_1m, masks)]
    later = [_mm(x, tri) for x in log_1m]
    a = [jnp.exp(log_beta[i] + later[i] + carries[i]) for i in n]
    a = [x if m is None else jnp.where(m, x, 0.0) for x, m in zip(a, masks)]
    outs = [_mm(a[i], vts[i], _NT) for i in n]
    return outs, [carries[i] + later[i][:, :1] + log_1m[i][:, :1] for i in n]


def _any_alive(kends, carries):
    flags = [jnp.logical_and(k > 0, jnp.max(c) > DEAD_LOG) for k, c in zip(kends, carries)]
    return functools.reduce(jnp.logical_or, flags)


def _sb_prompt_kernel(q_ref, kt_ref, vt_ref, tri_ref, o_ref):
    step = pl.program_id(2)
    t = q_ref.shape[1]
    N, Q, W, G = HEAD_DIM, SB_QBLOCK, SB_WINDOW, SB_QBLOCKS_PER_STEP
    heads = range(LANES // N)
    tri = tri_ref[...]

    def window(ws, hd):
        return kt_ref[0, hd, :, pl.ds(ws, W)], vt_ref[0, hd, :, pl.ds(ws, W)]

    def earlier_windows(qs, chain_heads, kends, accs, carries):
        col = lax.broadcasted_iota(jnp.int32, (qs[0].shape[0], W), 1)

        def body(st):
            kends, accs, carries = st
            ws = [pl.multiple_of(jnp.maximum(k - W, 0), Q) for k in kends]
            kts, vts = zip(*[window(s, hd) for s, hd in zip(ws, chain_heads)])
            masks = [s + col < k for s, k in zip(ws, kends)]
            parts, carries = _sb_windows(qs, kts, vts, tri, carries, masks)
            return ws, [a + p for a, p in zip(accs, parts)], carries

        return lax.while_loop(lambda st: _any_alive(st[0], st[2]), body, (kends, accs, carries))[1]

    row = lax.broadcasted_iota(jnp.int32, (Q, W), 0)
    col = lax.broadcasted_iota(jnp.int32, (Q, W), 1)
    q0 = [pl.multiple_of((step * G + j) * Q, Q) for j in range(G)]
    ws = [pl.multiple_of(jnp.maximum(x - (W - Q), 0), Q) for x in q0]
    chains = [(j, hd) for j in range(G) for hd in heads]
    chain_heads = [hd for _, hd in chains]
    qs = [q_ref[0, pl.ds(q0[j], Q), hd * N:(hd + 1) * N] for j, hd in chains]
    kts, vts = zip(*[window(ws[j], hd) for j, hd in chains])
    masks = [ws[j] + col < q0[j] + row for j, _ in chains]
    zero = jnp.zeros((Q, 1), f32)
    accs, carries = _sb_windows(qs, kts, vts, tri, [zero] * len(chains), masks)
    accs = earlier_windows(qs, chain_heads, [ws[j] for j, _ in chains], accs, carries)
    for j in range(G):
        o_ref[0, pl.ds(q0[j], Q), :] = jnp.concatenate([accs[j * len(heads) + hd] for hd in heads], axis=1)

    q_full = (t // Q) * Q
    rem = t - q_full
    if rem:
        @pl.when(step == pl.num_programs(2) - 1)
        def _():
            qs = [q_ref[0, q_full:, hd * N:(hd + 1) * N] for hd in heads]
            kts = [kt_ref[0, hd, :, q_full:] for hd in heads]
            vts = [vt_ref[0, hd, :, q_full:] for hd in heads]
            causal = (lax.broadcasted_iota(jnp.int32, (rem, rem), 1)
                      < lax.broadcasted_iota(jnp.int32, (rem, rem), 0))
            zero = jnp.zeros((rem, 1), f32)
            accs, carries = _sb_windows(qs, kts, vts, tri[:rem, :rem], [zero] * len(heads), [causal] * len(heads))
            accs = earlier_windows(qs, list(heads), [jnp.int32(q_full)] * len(heads), accs, carries)
            o_ref[0, q_full:, :] = jnp.concatenate(accs, axis=1)


def _sb_prompt(q, kt, vt):
    b, t, d = q.shape
    n_full = t // SB_QBLOCK
    assert n_full % SB_QBLOCKS_PER_STEP == 0 and t >= SB_WINDOW
    tri = _later_matrix(SB_WINDOW)
    qspec = pl.BlockSpec((1, t, LANES), lambda bi, hp, s: (bi, 0, hp))
    kvspec = pl.BlockSpec((1, LANES // HEAD_DIM, HEAD_DIM, t), lambda bi, hp, s: (bi, hp, 0, 0))
    return pl.pallas_call(
        _sb_prompt_kernel,
        grid=(b, d // LANES, n_full // SB_QBLOCKS_PER_STEP),
        in_specs=[qspec, kvspec, kvspec, _resident(tri.shape)],
        out_specs=qspec,
        out_shape=jax.ShapeDtypeStruct((b, t, d), f32),
        compiler_params=_params(("parallel", "parallel", "arbitrary")),
        name="sb_prompt",
    )(q, kt, vt, tri)


def _sb_sample_kernel(q_ref, ktn_ref, vtn_ref, ktp_ref, vtp_ref, kt_hbm, vt_hbm, tri_ref, o_ref, kbuf, vbuf, sem):
    bi, g = pl.program_id(0), pl.program_id(1)
    N = HEAD_DIM
    hg, t, wp = ktn_ref.shape[1], q_ref.shape[1], ktp_ref.shape[3]
    heads = range(hg)
    tri = tri_ref[...]

    qs = [q_ref[0, :, hd * N:(hd + 1) * N] for hd in heads]
    kts = [jnp.concatenate([ktp_ref[0, hd], ktn_ref[0, hd]], axis=1) for hd in heads]
    vts = [jnp.concatenate([vtp_ref[0, hd], vtn_ref[0, hd]], axis=1) for hd in heads]
    row = lax.broadcasted_iota(jnp.int32, (t, wp + t), 0)
    col = lax.broadcasted_iota(jnp.int32, (t, wp + t), 1)
    mask = col - wp < row
    accs, carries = _sb_windows(qs, kts, vts, tri, [jnp.zeros((t, 1), f32)] * hg, [mask] * hg)

    def fetch(src, dst, ws, slot):
        return pltpu.make_async_copy(src.at[bi, pl.ds(g * hg, hg), :, pl.ds(ws, wp)], dst, sem.at[slot])

    def body(st):
        kend, accs, carries = st
        ws = pl.multiple_of(kend - wp, wp)
        copies = [fetch(kt_hbm, kbuf, ws, 0), fetch(vt_hbm, vbuf, ws, 1)]
        for c in copies:
            c.start()
        for c in copies:
            c.wait()
        parts, carries = _sb_windows(qs, [kbuf[hd] for hd in heads], [vbuf[hd] for hd in heads],
                                     tri[:wp, :wp], carries, [None] * hg)
        return ws, [a + p for a, p in zip(accs, parts)], carries

    kend = jnp.int32(kt_hbm.shape[3] - wp)
    accs = lax.while_loop(lambda st: _any_alive([st[0]] * hg, st[2]), body, (kend, accs, carries))[1]
    o_ref[0] = jnp.concatenate(accs, axis=1)


def _sb_sample(q, kt, vt, kt_past, vt_past):
    b, t, d = q.shape
    hg, n, wp = SB_SAMPLE_HEADS, HEAD_DIM, SB_PAST_WINDOW
    p = kt_past.shape[3]
    assert p % wp == 0
    tri = _later_matrix(wp + t)
    qspec = pl.BlockSpec((1, t, hg * n), lambda bi, g: (bi, 0, g))
    new = pl.BlockSpec((1, hg, n, t), lambda bi, g: (bi, g, 0, 0))
    last = pl.BlockSpec((1, hg, n, wp), lambda bi, g: (bi, g, 0, p // wp - 1))
    hbm = pl.BlockSpec(memory_space=pl.ANY)
    return pl.pallas_call(
        _sb_sample_kernel,
        grid=(b, N_HEADS // hg),
        in_specs=[qspec, new, new, last, last, hbm, hbm, _resident(tri.shape)],
        out_specs=qspec,
        out_shape=jax.ShapeDtypeStruct((b, t, d), f32),
        scratch_shapes=[pltpu.VMEM((hg, n, wp), f32), pltpu.VMEM((hg, n, wp), f32), pltpu.SemaphoreType.DMA((2,))],
        compiler_params=_params(("parallel", "parallel")),
        name="sb_sample",
    )(q, kt, vt, kt_past, vt_past, kt_past, vt_past, tri)


def _grouped_kv_t(w_kv, heads_per_group):
    d = w_kv.shape[0]
    gw = heads_per_group * HEAD_DIM
    k_t = w_kv[:, :d].T.reshape(d // gw, gw, d)
    v_t = w_kv[:, d:].T.reshape(d // gw, gw, d)
    return jnp.concatenate([k_t, v_t], axis=1)


def _stack(x, s0, shift0, kt_past, vt_past, w, tm, tt, bs=1, head=None, tm_last=None):
    b, t0, d = x.shape
    skip = 0 if head is None else head.shape[0]
    t = t0 + skip
    m = b * t
    flat = lambda a: a.reshape(1, m, d)

    def ffn(x2, i, j, mix=None):
        if mix is not None:
            mix = (flat(mix[0]), None if mix[1] is None else flat(mix[1]), mix[2])
        return _ffn(flat(x2), w["ffn_g"][i][j], w["ffn_wg"], w["ffn_wu"], w["ffn_wo"], (i, j),
                    tm, mix).reshape(m, d)

    if head is None:
        x2 = ffn(x.reshape(m, d), 0, 0)
    else:
        x2 = _ffn(x, w["ffn_g"][0][0], w["ffn_wg"], w["ffn_wu"], w["ffn_wo"], (0, 0), tm,
                  head=head).reshape(m, d)
    rw = w["rwkv"]
    r, lw, k, v, kk, bb, g, shift = _rwkv_pre(x2.reshape(b, t, d), shift0.reshape(b, 1, d), w["mix_g"][0], rw,
                                              w["eseg"], w["eexp"], tt, bs)
    y, s_fin = _rwkv_scan(r, lw, k, v, kk, bb, s0, rw, w["tri_chunk"])
    x2 = ffn(x2, 0, 1, (y, g, rw["w_o"]))

    x2 = ffn(x2, 1, 0)
    sb = w["sb"]
    short = t <= 2 * LANES
    q, kt, vt = _sb_pre(x2.reshape(b, t, d), w["mix_g"][1], sb["w_q"], sb["w_kv_t_all" if short else "w_kv_t"],
                        sb["q_g"], sb["k_g_col"], w["eseg"], w["eexp"], streams_inner=short)
    if kt_past is None:
        o = _sb_prompt(q, kt, vt)
    else:
        o = _sb_sample(q, kt, vt, kt_past, vt_past)
    if skip:
        y = _ffn(x2.reshape(b, t, d), w["ffn_g"][1][1], w["ffn_wg"], w["ffn_wu"], w["ffn_wo"], (1, 1),
                 tm_last, (o, None, sb["w_o"]), skip)
    else:
        y = ffn(x2, 1, 1, (o, None, sb["w_o"])).reshape(b, t, d)
    return y, s_fin, shift.reshape(b, d), kt, vt


def kernel(x_prompt, x_sample, state_rwkv_wkv, state_rwkv_shift, cache_sb_k, cache_sb_v, meta_tokens,
           ffn_norm_g, ffn_w_in, ffn_w_out, mix_norm_g, rwkv_mu, rwkv_w_rkv, rwkv_w0, rwkv_w1, rwkv_w2,
           rwkv_a0, rwkv_a1, rwkv_a2, rwkv_g1, rwkv_g2, rwkv_k_k, rwkv_k_a, rwkv_r_k, rwkv_gn_w, rwkv_gn_b,
           rwkv_w_o, sb_w_qkv, sb_q_norm_g, sb_k_norm_g, sb_w_o):
    d = D_MODEL
    depth = ffn_w_in.shape[0]
    d_ff = ffn_w_out.shape[2]
    w_gate = ffn_w_in[..., :d_ff].astype(bf16)
    w_up = ffn_w_in[..., d_ff:].astype(bf16)
    w_out = ffn_w_out.astype(bf16)
    vec = lambda a: a.reshape(1, d).astype(f32)
    head_of_lane = jnp.arange(d) // HEAD_DIM
    eseg = (head_of_lane[:, None] == jnp.arange(LANES)[None, :]).astype(bf16)
    w = {
        "ffn_g": [[vec(ffn_norm_g[i, j]) for j in range(2)] for i in range(depth)],
        "ffn_wg": w_gate, "ffn_wu": w_up, "ffn_wo": w_out,
        "mix_g": [vec(mix_norm_g[i]) for i in range(depth)],
        "eseg": eseg, "eexp": eseg.T,
        "tri_chunk": (jnp.arange(CHUNK)[:, None] >= jnp.arange(CHUNK)[None, :]).astype(bf16),
        "rwkv": {
            "mu": rwkv_mu, "w_rkv": rwkv_w_rkv.astype(bf16), "w0": vec(rwkv_w0), "w1": rwkv_w1.astype(bf16),
            "w2": rwkv_w2.astype(bf16), "a0": vec(rwkv_a0), "a1": rwkv_a1.astype(bf16), "a2": rwkv_a2.astype(bf16),
            "g1": rwkv_g1.astype(bf16), "g2": rwkv_g2.astype(bf16), "k_k": vec(rwkv_k_k), "k_a": vec(rwkv_k_a),
            "r_k": vec(rwkv_r_k), "gn_w": vec(rwkv_gn_w), "gn_b": vec(rwkv_gn_b), "w_o": rwkv_w_o.astype(bf16),
        },
        "sb": {
            "w_q": sb_w_qkv[:, :d].astype(bf16), "w_kv_t": _grouped_kv_t(sb_w_qkv[:, d:], SB_PRE_HEADS).astype(bf16),
            "w_kv_t_all": _grouped_kv_t(sb_w_qkv[:, d:], N_HEADS).astype(bf16),
            "q_g": vec(jnp.tile(sb_q_norm_g, N_HEADS)), "k_g_col": sb_k_norm_g.reshape(HEAD_DIM, 1).astype(f32),
            "w_o": sb_w_o.astype(bf16),
        },
    }

    bp = x_prompt.shape[0]
    s0_p = jnp.zeros((bp, N_HEADS, HEAD_DIM, HEAD_DIM), f32)
    shift0_p = jnp.zeros((bp, d), f32)
    y_prompt, s_p, shift_p, kt_p, vt_p = _stack(x_prompt, s0_p, shift0_p, None, None, w, tm=688, tt=344,
                                                head=meta_tokens.astype(f32), tm_last=512)
    frames_last = lambda a: jnp.swapaxes(a, 2, 3)
    ys, s_s, shift_s, kt_s, vt_s = _stack(x_sample, state_rwkv_wkv, state_rwkv_shift, frames_last(cache_sb_k),
                                          frames_last(cache_sb_v), w, tm=512, tt=64, bs=4)
    return (y_prompt, ys, s_p, shift_p, frames_last(kt_p), frames_last(vt_p), s_s, shift_s,
            frames_last(kt_s), frames_last(vt_s))
```

```python
import functools

import jax
import jax.numpy as jnp
from jax import lax
from jax.experimental import pallas as pl
from jax.experimental.pallas import tpu as pltpu

D_MODEL = 1024
HEAD_DIM = 64
N_HEADS = D_MODEL // HEAD_DIM
RMS_EPS = 1e-6
GN_EPS = 64e-5

LANES = 128
CHUNK = 64
SCAN_CHUNKS_PER_STEP = 3
FF_CHUNK = 256
DEAD_LOG = -104.0
VMEM_LIMIT = 52 * 1024 * 1024

f32 = jnp.float32
bf16 = jnp.bfloat16

_NN = (((1,), (0,)), ((), ()))
_NT = (((1,), (1,)), ((), ()))
_TN = (((0,), (0,)), ((), ()))


def _mm(a, b, dims=_NN):
    return lax.dot_general(a.astype(bf16), b.astype(bf16), dims, preferred_element_type=f32)


def _split(a):
    hi = a.astype(bf16)
    lo = (a - hi.astype(f32)).astype(bf16)
    return hi, lo


def _mm3(a, b, dims=_NN):
    ah, al = _split(a)
    bh, bl = _split(b)
    dot = functools.partial(lax.dot_general, dimension_numbers=dims, preferred_element_type=f32)
    return dot(ah, bh) + (dot(ah, bl) + dot(al, bh))


def _mm_exact_rhs(a, b_bf16, dims=_NN):
    ah, al = _split(a)
    dot = functools.partial(lax.dot_general, dimension_numbers=dims, preferred_element_type=f32)
    return dot(ah, b_bf16) + dot(al, b_bf16)


def _mm_exact_lhs(a_bf16, b, dims=_NN):
    bh, bl = _split(b)
    dot = functools.partial(lax.dot_general, dimension_numbers=dims, preferred_element_type=f32)
    return dot(a_bf16, bh) + dot(a_bf16, bl)


def _rms(x, g):
    return x * lax.rsqrt(jnp.mean(x * x, axis=-1, keepdims=True) + RMS_EPS) * g


def _softplus(y):
    return jnp.maximum(y, 0.0) + jnp.log1p(jnp.exp(-jnp.abs(y)))


def _sigmoid(y):
    return 1.0 / (1.0 + jnp.exp(-y))


def _head_sum(x, eseg_ref, eexp_ref):
    return _mm_exact_rhs(x, eseg_ref[...])


def _head_expand(y, eexp_ref):
    return _mm_exact_rhs(y, eexp_ref[...])


def _full(shape):
    return pl.BlockSpec(shape, lambda *_: (0,) * len(shape))


def _resident(shape):
    return pl.BlockSpec(shape, lambda *_: (0,) * len(shape), pipeline_mode=pl.Buffered(1))


def _params(sem):
    return pltpu.CompilerParams(dimension_semantics=sem, vmem_limit_bytes=VMEM_LIMIT)


def _ffn_kernel(n_mix, n_head, *refs):
    mix, refs = refs[:n_mix], refs[n_mix:]
    head, (x_ref, g_ref, wg_ref, wu_ref, wo_ref, o_ref, acc_ref) = refs[:bool(n_head)], refs[bool(n_head):]
    rows = lambda ref: ref[...].reshape(ref.shape[-2:])
    x = rows(x_ref)
    if n_head:
        moved = jnp.concatenate([head[0][...], x[:x.shape[0] - n_head]], axis=0)
        x = jnp.where(pl.program_id(1) == 0, moved, x)
    if n_mix:
        y = rows(mix[0])
        if n_mix == 3:
            y = y * rows(mix[1])
        x = x + jnp.dot(y.astype(bf16), mix[-1][...], preferred_element_type=f32)
    h = _rms(x, g_ref[...]).astype(bf16)
    d_ff = wg_ref.shape[1]
    for c in range(pl.cdiv(d_ff, FF_CHUNK)):
        sl = slice(c * FF_CHUNK, min((c + 1) * FF_CHUNK, d_ff))
        gate = jnp.dot(h, wg_ref[:, sl], preferred_element_type=f32)
        up = jnp.dot(h, wu_ref[:, sl], preferred_element_type=f32)
        act = (gate * _sigmoid(gate) * up).astype(bf16)
        part = jnp.dot(act, wo_ref[sl, :], preferred_element_type=f32)
        if c == 0:
            acc_ref[...] = part
        else:
            acc_ref[...] += part
    o_ref[...] = x + 0.5 * acc_ref[...]


def _ffn(x, g, wg, wu, wo, ij, tm, mix=None, skip=0, head=None):
    b, t, d = x.shape
    fp = wg.shape[-1]
    picked = lambda *shape: pl.BlockSpec((None, None) + shape, lambda *_: (*ij, 0, 0), pipeline_mode=pl.Buffered(1))
    n_head = 0 if head is None else head.shape[0]
    assert not (n_head and (skip or mix is not None))
    t_out = t - skip + n_head
    assert t_out % tm == 0 and skip % 8 == 0 and n_head % 8 == 0 and tm % 8 == 0
    if skip or n_head:
        row_in = pl.BlockSpec(
            (pl.Element(1), pl.Element(tm), pl.Element(d)),
            lambda bi, i: (bi, pl.multiple_of(jnp.maximum(skip - n_head + i * tm, 0), 8), 0))
    else:
        row_in = pl.BlockSpec((None, tm, d), lambda bi, i: (bi, i, 0))
    row_out = pl.BlockSpec((None, tm, d), lambda bi, i: (bi, i, 0))
    mix_ins, mix_specs = [], []
    if mix is not None:
        y, gate, w_proj = mix
        mix_ins = [y] + ([gate] if gate is not None else []) + [w_proj]
        mix_specs = [row_in] * (len(mix_ins) - 1) + [_resident(w_proj.shape)]
    head_ins, head_specs = ([head], [_resident(head.shape)]) if n_head else ([], [])
    return pl.pallas_call(
        functools.partial(_ffn_kernel, len(mix_ins), n_head),
        grid=(b, t_out // tm),
        in_specs=(mix_specs + head_specs
                  + [row_in, _full((1, d)), picked(d, fp), picked(d, fp), picked(fp, d)]),
        out_specs=row_out,
        out_shape=jax.ShapeDtypeStruct((b, t_out, d), f32),
        scratch_shapes=[pltpu.VMEM((tm, d), f32)],
        compiler_params=_params(("parallel", "parallel")),
        name="ffn",
    )(*mix_ins, *head_ins, x, g, wg, wu, wo)


def _rwkv_pre_kernel(x_ref, shift_ref, ng_ref, mu_ref, wrkv_ref, w0_ref, w1_ref, w2_ref,
                     a0_ref, a1_ref, a2_ref, g1_ref, g2_ref, kk_ref, ka_ref, eseg_ref, eexp_ref,
                     r_out, lw_out, k_out, v_out, kk_out, b_out, g_out, shift_out, prev_ref):
    i = pl.program_id(1)
    bs, tt, d = x_ref.shape

    @pl.when(i == 0)
    def _():
        prev_ref[...] = shift_ref[...]

    h = _rms(x_ref[...].reshape(bs * tt, d), ng_ref[...])
    first_row = lax.broadcasted_iota(jnp.int32, (bs, tt, d), 1).reshape(bs * tt, d) == 0
    carried = jnp.broadcast_to(prev_ref[...], (bs, tt, d)).reshape(bs * tt, d)
    h_prev = jnp.where(first_row, carried, pltpu.roll(h, 1, axis=0))
    last = h.reshape(bs, tt, d)[:, tt - 1:tt, :]
    prev_ref[...] = last
    shift_out[...] = last

    dx = h_prev - h
    xr, xw, xk, xv, xa, xg = ((h + dx * mu_ref[j:j + 1, :]).astype(bf16) for j in range(6))
    r = jnp.dot(xr, wrkv_ref[0], preferred_element_type=f32)
    k = jnp.dot(xk, wrkv_ref[1], preferred_element_type=f32)
    v = jnp.dot(xv, wrkv_ref[2], preferred_element_type=f32)
    lora_w = _mm(jnp.tanh(jnp.dot(xw, w1_ref[...], preferred_element_type=f32)), w2_ref[...])
    log_rate = -_softplus(-(w0_ref[...] + lora_w)) - 0.5
    a = _sigmoid(a0_ref[...] + _mm(jnp.dot(xa, a1_ref[...], preferred_element_type=f32), a2_ref[...]))
    g = _mm(_sigmoid(jnp.dot(xg, g1_ref[...], preferred_element_type=f32)), g2_ref[...])

    kk = k * kk_ref[...]
    norm = jnp.sqrt(_head_sum(kk * kk, eseg_ref, eexp_ref))
    kk = kk * _head_expand(1.0 / jnp.maximum(norm, 1e-12), eexp_ref)

    outs = (r, -jnp.exp(log_rate), k * (1.0 + (a - 1.0) * ka_ref[...]), v, kk, kk * a, g)
    for ref, val in zip((r_out, lw_out, k_out, v_out, kk_out, b_out, g_out), outs):
        ref[...] = val.reshape(bs, tt, d)


def _rwkv_pre(x, shift0, ng, p, eseg, eexp, tt, bs):
    b, t, d = x.shape
    seq = pl.BlockSpec((bs, tt, d), lambda bi, i: (bi, i, 0))
    vec = pl.BlockSpec((bs, 1, d), lambda bi, i: (bi, 0, 0))
    out = jax.ShapeDtypeStruct((b, t, d), f32)
    ins = [x, shift0, ng, p["mu"], p["w_rkv"], p["w0"], p["w1"], p["w2"], p["a0"], p["a1"], p["a2"],
           p["g1"], p["g2"], p["k_k"], p["k_a"], eseg, eexp]
    specs = [seq, vec] + [_resident(a.shape) for a in ins[2:]]
    return pl.pallas_call(
        _rwkv_pre_kernel,
        grid=(b // bs, t // tt),
        in_specs=specs,
        out_specs=[seq] * 7 + [vec],
        out_shape=[out] * 7 + [jax.ShapeDtypeStruct((b, 1, d), f32)],
        scratch_shapes=[pltpu.VMEM((bs, 1, d), f32)],
        compiler_params=_params(("parallel", "arbitrary")),
        name="rwkv_pre",
    )(*ins)


def _rwkv_scan_kernel(t_total, n_sub, r_ref, lw_ref, k_ref, v_ref, kk_ref, b_ref, s0_ref, rk_ref, gnw_ref, gnb_ref,
                      tri_ref, y_ref, s_ref, sbd_ref):
    step = pl.program_id(1)
    C, N = CHUNK, HEAD_DIM
    per_tile = LANES // N
    pairs = range(N_HEADS // per_tile)
    tiles = [slice(p * LANES, (p + 1) * LANES) for p in pairs]

    @pl.when(step == 0)
    def _():
        zero = jnp.zeros((N, N), f32)
        for p in pairs:
            sbd_ref[p] = jnp.concatenate(
                [jnp.concatenate([s0_ref[0, per_tile * p], zero], axis=1),
                 jnp.concatenate([zero, s0_ref[0, per_tile * p + 1]], axis=1)], axis=0)

    row = lax.broadcasted_iota(jnp.int32, (C, 2 * C), 0)
    lane = lax.broadcasted_iota(jnp.int32, (C, 2 * C), 1)
    left = lane < C
    col = jnp.where(left, lane, lane - C)
    strict = row > col
    incl = row >= col
    eye_right = jnp.logical_and(row == col, jnp.logical_not(left)).astype(f32)
    head_lanes = [(lax.broadcasted_iota(jnp.int32, (2 * C, LANES), 1) // N) == s for s in range(per_tile)]
    first = lax.broadcasted_iota(jnp.int32, (C, LANES), 1) < N

    chunks = []
    for j in range(n_sub):
        rows = slice(j * C, (j + 1) * C)
        r, lw, k, v, kk, b = (ref[0, rows] for ref in (r_ref, lw_ref, k_ref, v_ref, kk_ref, b_ref))
        if t_total % (n_sub * C):
            valid = (step * n_sub + j) * C + lax.broadcasted_iota(jnp.int32, r.shape, 0) < t_total
            r, lw, k, v, kk, b = (jnp.where(valid, a, 0.0) for a in (r, lw, k, v, kk, b))
        cum = _mm_exact_lhs(tri_ref[...], lw)
        p_t = jnp.exp(cum)
        p_inv = jnp.exp(-cum)
        r_t = r * p_t
        a_t = -kk * jnp.exp(cum - lw)
        b_t = b * p_inv
        k_t = k * p_inv
        gamma = p_t[C - 1:C, :]
        b_g = b_t * gamma
        k_g = k_t * gamma
        chunks.append(dict(
            rows=rows, gamma=gamma, bonus=r * k * rk_ref[...],
            ar=[jnp.concatenate([a_t[:, sl], r_t[:, sl]], axis=0) for sl in tiles],
            bk=[jnp.concatenate([b_t[:, sl], k_t[:, sl]], axis=0) for sl in tiles],
            bkg=[jnp.concatenate([b_g[:, sl], k_g[:, sl]], axis=0) for sl in tiles],
            v=[v[:, sl] for sl in tiles]))

    chains = [(j, p, s) for j in range(n_sub) for p in pairs for s in range(per_tile)]
    ar = [jnp.where(head_lanes[s], chunks[j]["ar"][p], 0.0) for j, p, s in chains]
    x2 = [_mm(ar[i], chunks[j]["bk"][p], _NT) for i, (j, p, _) in enumerate(chains)]
    z = [jnp.where(jnp.logical_and(strict, left), x[:C], 0.0) + eye_right for x in x2]
    for _ in range(6):
        sq = [_mm(x[:, :C], x) for x in z]
        z = [jnp.where(left, q, x + q) for x, q in zip(z, sq)]
    t_inv = [jnp.where(left, 0.0, x) for x in z]
    a_ak = [jnp.where(jnp.logical_and(strict, jnp.logical_not(left)), x[:C], 0.0) for x in x2]
    akv = [_mm(a_ak[i], jnp.concatenate([chunks[j]["v"][p]] * 2, axis=0)) for i, (j, p, _) in enumerate(chains)]
    a_r = [jnp.where(incl, x[C:], 0.0) for x in x2]

    def head_sum(x):
        s0 = jnp.sum(jnp.where(first, x, 0.0), axis=-1, keepdims=True)
        s1 = jnp.sum(jnp.where(first, 0.0, x), axis=-1, keepdims=True)
        return jnp.where(first, s0, s1)

    for j in range(n_sub):
        ck = chunks[j]
        mine = [i for i, (jj, _, _) in enumerate(chains) if jj == j]
        s_old = [sbd_ref[p] for p in pairs]
        xs = {i: _mm_exact_lhs(ar[i].astype(bf16), s_old[chains[i][1]], _NT) for i in mine}
        rhs = {i: xs[i][:C] + akv[i] for i in mine}
        u = {i: _mm(t_inv[i], jnp.concatenate([rhs[i], rhs[i]], axis=0)) for i in mine}
        uv = {i: jnp.concatenate([u[i], ck["v"][chains[i][1]]], axis=0) for i in mine}
        o = {i: xs[i][C:] + _mm(a_r[i], uv[i]) for i in mine}
        s_add = {i: _mm(uv[i], jnp.where(head_lanes[chains[i][2]], ck["bkg"][chains[i][1]], 0.0), _TN) for i in mine}
        y_tiles = []
        for p in pairs:
            i0, sl = mine[per_tile * p], tiles[p]
            sbd_ref[p] = (s_old[p] * ck["gamma"][:, sl]
                          + jnp.concatenate([s_add[i0][:N], s_add[i0 + 1][N:]], axis=0))
            o_pair = jnp.where(first, o[i0], o[i0 + 1])
            dev = o_pair - head_sum(o_pair) * (1.0 / N)
            o_n = dev * lax.rsqrt(head_sum(dev * dev) * (1.0 / N) + GN_EPS) * gnw_ref[:, sl] + gnb_ref[:, sl]
            y_tiles.append(o_n + head_sum(ck["bonus"][:, sl]) * ck["v"][p])
        y_ref[0, ck["rows"]] = jnp.concatenate(y_tiles, axis=1)

    @pl.when(step == pl.num_programs(1) - 1)
    def _():
        for p in pairs:
            s_ref[0, per_tile * p] = sbd_ref[p][:N, :N]
            s_ref[0, per_tile * p + 1] = sbd_ref[p][N:, N:]


def _rwkv_scan(r, lw, k, v, kk, b, s0, p, tri):
    bsz, t, d = r.shape
    n_sub = min(SCAN_CHUNKS_PER_STEP, pl.cdiv(t, CHUNK))
    seq = pl.BlockSpec((1, n_sub * CHUNK, d), lambda bi, c: (bi, c, 0))
    st = pl.BlockSpec((1, N_HEADS, HEAD_DIM, HEAD_DIM), lambda bi, c: (bi, 0, 0, 0))
    consts = [p["r_k"], p["gn_w"], p["gn_b"], tri]
    return pl.pallas_call(
        functools.partial(_rwkv_scan_kernel, t, n_sub),
        grid=(bsz, pl.cdiv(t, n_sub * CHUNK)),
        in_specs=[seq] * 6 + [st] + [_resident(a.shape) for a in consts],
        out_specs=[seq, st],
        out_shape=[jax.ShapeDtypeStruct((bsz, t, d), f32), jax.ShapeDtypeStruct(s0.shape, f32)],
        scratch_shapes=[pltpu.VMEM((N_HEADS * HEAD_DIM // LANES, LANES, LANES), f32)],
        compiler_params=_params(("parallel", "arbitrary")),
        name="rwkv_scan",
    )(r, lw, k, v, kk, b, s0, *consts)


SB_PRE_HEADS = 4


def _sb_pre_kernel(reuse_h, x_ref, ng_ref, wq_ref, wkv_ref, qg_ref, kg_ref, eseg_ref, eexp_ref, q_out, kt_out, vt_out,
                   h_ref):
    hg, n = kt_out.shape[1], HEAD_DIM
    t = x_ref.shape[1]

    def normed():
        return _rms(x_ref[0], ng_ref[...]).astype(bf16)

    if reuse_h:
        @pl.when(pl.program_id(1) == 0)
        def _():
            h_ref[...] = normed()
        h = h_ref[...]
    else:
        h = normed()
    q = jnp.dot(h, wq_ref[...], preferred_element_type=f32)
    ms = _head_sum(q * q, eseg_ref, eexp_ref) * (1.0 / n)
    q_out[0] = q * _head_expand(lax.rsqrt(ms + RMS_EPS), eexp_ref) * qg_ref[...]

    kv = lax.dot_general(wkv_ref[0], h, _NT, preferred_element_type=f32)
    k = kv[:hg * n].reshape(hg, n, t)
    ms = jnp.mean(k * k, axis=1, keepdims=True)
    kt_out[0] = k * lax.rsqrt(ms + RMS_EPS) * kg_ref[...]
    vt_out[0] = kv[hg * n:].reshape(hg, n, t)


def _sb_pre(x, ng, wq, wkv_t, qg, kg_col, eseg, eexp, streams_inner):
    b, t, d = x.shape
    n = HEAD_DIM
    w = wkv_t.shape[1] // 2
    hg = w // n
    ins = [x, ng, wq, wkv_t, qg, kg_col, eseg[:w], eexp[:, :w]]
    if streams_inner:
        grid = (N_HEADS // hg, b)
        ix = lambda f: (lambda g, bi: f(bi, g))
    else:
        grid = (b, N_HEADS // hg)
        ix = lambda f: f
    tspec = pl.BlockSpec((1, hg, n, t), ix(lambda bi, g: (bi, g, 0, 0)))
    tout = jax.ShapeDtypeStruct((b, N_HEADS, n, t), f32)
    return pl.pallas_call(
        functools.partial(_sb_pre_kernel, not streams_inner),
        grid=grid,
        in_specs=[
            pl.BlockSpec((1, t, d), ix(lambda bi, g: (bi, 0, 0))),
            _resident(ng.shape),
            pl.BlockSpec((d, w), ix(lambda bi, g: (0, g))),
            pl.BlockSpec((1, 2 * w, d), ix(lambda bi, g: (g, 0, 0))),
            pl.BlockSpec((1, w), ix(lambda bi, g: (0, g))),
            _resident(kg_col.shape),
            _resident((w, LANES)),
            _resident((LANES, w)),
        ],
        out_specs=[pl.BlockSpec((1, t, w), ix(lambda bi, g: (bi, 0, g))), tspec, tspec],
        out_shape=[jax.ShapeDtypeStruct((b, t, d), f32), tout, tout],
        scratch_shapes=[pltpu.VMEM((t, d), bf16)],
        compiler_params=_params(("arbitrary", "arbitrary")),
        name="sb_pre",
    )(*ins)


SB_QBLOCK = 128
SB_WINDOW = 3 * SB_QBLOCK
SB_QBLOCKS_PER_STEP = 8
SB_PAST_WINDOW = 256
SB_SAMPLE_HEADS = 8


def _later_matrix(n):
    i = jnp.arange(n)
    return (i[:, None] > i[None, :]).astype(bf16)


def _sb_windows(qs, kts, vts, tri, carries, masks):
    n = range(len(qs))
    z = [_mm(qs[i], kts[i]) * (HEAD_DIM ** -0.5) for i in n]
    log_beta = [jnp.minimum(x, 0.0) - jnp.log(1.0 + jnp.exp(-jnp.abs(x))) for x in z]
    log_1m = [log_beta[i] - z[i] for i in n]
    log_1m = [x if m is None else jnp.where(m, x, 0.0) for x, m in zip(log_1m, masks)]
    later = [_mm(x, tri) for x in log_1m]
    a = [jnp.exp(log_beta[i] + later[i] + carries[i]) for i in n]
    a = [x if m is None else jnp.where(m, x, 0.0) for x, m in zip(a, masks)]
    outs = [_mm(a[i], vts[i], _NT) for i in n]
    return outs, [carries[i] + later[i][:, :1] + log_1m[i][:, :1] for i in n]


def _any_alive(kends, carries):
    flags = [jnp.logical_and(k > 0, jnp.max(c) > DEAD_LOG) for k, c in zip(kends, carries)]
    return functools.reduce(jnp.logical_or, flags)


def _sb_prompt_kernel(q_ref, kt_ref, vt_ref, tri_ref, o_ref):
    step = pl.program_id(2)
    t = q_ref.shape[1]
    N, Q, W, G = HEAD_DIM, SB_QBLOCK, SB_WINDOW, SB_QBLOCKS_PER_STEP
    heads = range(LANES // N)
    tri = tri_ref[...]

    def window(ws, hd):
        return kt_ref[0, hd, :, pl.ds(ws, W)], vt_ref[0, hd, :, pl.ds(ws, W)]

    def earlier_windows(qs, chain_heads, kends, accs, carries):
        col = lax.broadcasted_iota(jnp.int32, (qs[0].shape[0], W), 1)

        def body(st):
            kends, accs, carries = st
            ws = [pl.multiple_of(jnp.maximum(k - W, 0), Q) for k in kends]
            kts, vts = zip(*[window(s, hd) for s, hd in zip(ws, chain_heads)])
            masks = [s + col < k for s, k in zip(ws, kends)]
            parts, carries = _sb_windows(qs, kts, vts, tri, carries, masks)
            return ws, [a + p for a, p in zip(accs, parts)], carries

        return lax.while_loop(lambda st: _any_alive(st[0], st[2]), body, (kends, accs, carries))[1]

    row = lax.broadcasted_iota(jnp.int32, (Q, W), 0)
    col = lax.broadcasted_iota(jnp.int32, (Q, W), 1)
    q0 = [pl.multiple_of((step * G + j) * Q, Q) for j in range(G)]
    ws = [pl.multiple_of(jnp.maximum(x - (W - Q), 0), Q) for x in q0]
    chains = [(j, hd) for j in range(G) for hd in heads]
    chain_heads = [hd for _, hd in chains]
    qs = [q_ref[0, pl.ds(q0[j], Q), hd * N:(hd + 1) * N] for j, hd in chains]
    kts, vts = zip(*[window(ws[j], hd) for j, hd in chains])
    masks = [ws[j] + col < q0[j] + row for j, _ in chains]
    zero = jnp.zeros((Q, 1), f32)
    accs, carries = _sb_windows(qs, kts, vts, tri, [zero] * len(chains), masks)
    accs = earlier_windows(qs, chain_heads, [ws[j] for j, _ in chains], accs, carries)
    for j in range(G):
        o_ref[0, pl.ds(q0[j], Q), :] = jnp.concatenate([accs[j * len(heads) + hd] for hd in heads], axis=1)

    q_full = (t // Q) * Q
    rem = t - q_full
    if rem:
        @pl.when(step == pl.num_programs(2) - 1)
        def _():
            qs = [q_ref[0, q_full:, hd * N:(hd + 1) * N] for hd in heads]
            kts = [kt_ref[0, hd, :, q_full:] for hd in heads]
            vts = [vt_ref[0, hd, :, q_full:] for hd in heads]
            causal = (lax.broadcasted_iota(jnp.int32, (rem, rem), 1)
                      < lax.broadcasted_iota(jnp.int32, (rem, rem), 0))
            zero = jnp.zeros((rem, 1), f32)
            accs, carries = _sb_windows(qs, kts, vts, tri[:rem, :rem], [zero] * len(heads), [causal] * len(heads))
            accs = earlier_windows(qs, list(heads), [jnp.int32(q_full)] * len(heads), accs, carries)
            o_ref[0, q_full:, :] = jnp.concatenate(accs, axis=1)


def _sb_prompt(q, kt, vt):
    b, t, d = q.shape
    n_full = t // SB_QBLOCK
    assert n_full % SB_QBLOCKS_PER_STEP == 0 and t >= SB_WINDOW
    tri = _later_matrix(SB_WINDOW)
    qspec = pl.BlockSpec((1, t, LANES), lambda bi, hp, s: (bi, 0, hp))
    kvspec = pl.BlockSpec((1, LANES // HEAD_DIM, HEAD_DIM, t), lambda bi, hp, s: (bi, hp, 0, 0))
    return pl.pallas_call(
        _sb_prompt_kernel,
        grid=(b, d // LANES, n_full // SB_QBLOCKS_PER_STEP),
        in_specs=[qspec, kvspec, kvspec, _resident(tri.shape)],
        out_specs=qspec,
        out_shape=jax.ShapeDtypeStruct((b, t, d), f32),
        compiler_params=_params(("parallel", "parallel", "arbitrary")),
        name="sb_prompt",
    )(q, kt, vt, tri)


def _sb_sample_kernel(q_ref, ktn_ref, vtn_ref, ktp_ref, vtp_ref, kt_hbm, vt_hbm, tri_ref, o_ref, kbuf, vbuf, sem):
    bi, g = pl.program_id(0), pl.program_id(1)
    N = HEAD_DIM
    hg, t, wp = ktn_ref.shape[1], q_ref.shape[1], ktp_ref.shape[3]
    heads = range(hg)
    tri = tri_ref[...]

    qs = [q_ref[0, :, hd * N:(hd + 1) * N] for hd in heads]
    kts = [jnp.concatenate([ktp_ref[0, hd], ktn_ref[0, hd]], axis=1) for hd in heads]
    vts = [jnp.concatenate([vtp_ref[0, hd], vtn_ref[0, hd]], axis=1) for hd in heads]
    row = lax.broadcasted_iota(jnp.int32, (t, wp + t), 0)
    col = lax.broadcasted_iota(jnp.int32, (t, wp + t), 1)
    mask = col - wp < row
    accs, carries = _sb_windows(qs, kts, vts, tri, [jnp.zeros((t, 1), f32)] * hg, [mask] * hg)

    def fetch(src, dst, ws, slot):
        return pltpu.make_async_copy(src.at[bi, pl.ds(g * hg, hg), :, pl.ds(ws, wp)], dst, sem.at[slot])

    def body(st):
        kend, accs, carries = st
        ws = pl.multiple_of(kend - wp, wp)
        copies = [fetch(kt_hbm, kbuf, ws, 0), fetch(vt_hbm, vbuf, ws, 1)]
        for c in copies:
            c.start()
        for c in copies:
            c.wait()
        parts, carries = _sb_windows(qs, [kbuf[hd] for hd in heads], [vbuf[hd] for hd in heads],
                                     tri[:wp, :wp], carries, [None] * hg)
        return ws, [a + p for a, p in zip(accs, parts)], carries

    kend = jnp.int32(kt_hbm.shape[3] - wp)
    accs = lax.while_loop(lambda st: _any_alive([st[0]] * hg, st[2]), body, (kend, accs, carries))[1]
    o_ref[0] = jnp.concatenate(accs, axis=1)


def _sb_sample(q, kt, vt, kt_past, vt_past):
    b, t, d = q.shape
    hg, n, wp = SB_SAMPLE_HEADS, HEAD_DIM, SB_PAST_WINDOW
    p = kt_past.shape[3]
    assert p % wp == 0
    tri = _later_matrix(wp + t)
    qspec = pl.BlockSpec((1, t, hg * n), lambda bi, g: (bi, 0, g))
    new = pl.BlockSpec((1, hg, n, t), lambda bi, g: (bi, g, 0, 0))
    last = pl.BlockSpec((1, hg, n, wp), lambda bi, g: (bi, g, 0, p // wp - 1))
    hbm = pl.BlockSpec(memory_space=pl.ANY)
    return pl.pallas_call(
        _sb_sample_kernel,
        grid=(b, N_HEADS // hg),
        in_specs=[qspec, new, new, last, last, hbm, hbm, _resident(tri.shape)],
        out_specs=qspec,
        out_shape=jax.ShapeDtypeStruct((b, t, d), f32),
        scratch_shapes=[pltpu.VMEM((hg, n, wp), f32), pltpu.VMEM((hg, n, wp), f32), pltpu.SemaphoreType.DMA((2,))],
        compiler_params=_params(("parallel", "parallel")),
        name="sb_sample",
    )(q, kt, vt, kt_past, vt_past, kt_past, vt_past, tri)


def _grouped_kv_t(w_kv, heads_per_group):
    d = w_kv.shape[0]
    gw = heads_per_group * HEAD_DIM
    k_t = w_kv[:, :d].T.reshape(d // gw, gw, d)
    v_t = w_kv[:, d:].T.reshape(d // gw, gw, d)
    return jnp.concatenate([k_t, v_t], axis=1)


def _stack(x, s0, shift0, kt_past, vt_past, w, tm, tt, bs=1, head=None, tm_last=None):
    b, t0, d = x.shape
    skip = 0 if head is None else head.shape[0]
    t = t0 + skip
    m = b * t
    flat = lambda a: a.reshape(1, m, d)

    def ffn(x2, i, j, mix=None):
        if mix is not None:
            mix = (flat(mix[0]), None if mix[1] is None else flat(mix[1]), mix[2])
        return _ffn(flat(x2), w["ffn_g"][i][j], w["ffn_wg"], w["ffn_wu"], w["ffn_wo"], (i, j),
                    tm, mix).reshape(m, d)

    if head is None:
        x2 = ffn(x.reshape(m, d), 0, 0)
    else:
        x2 = _ffn(x, w["ffn_g"][0][0], w["ffn_wg"], w["ffn_wu"], w["ffn_wo"], (0, 0), tm,
                  head=head).reshape(m, d)
    rw = w["rwkv"]
    r, lw, k, v, kk, bb, g, shift = _rwkv_pre(x2.reshape(b, t, d), shift0.reshape(b, 1, d), w["mix_g"][0], rw,
                                              w["eseg"], w["eexp"], tt, bs)
    y, s_fin = _rwkv_scan(r, lw, k, v, kk, bb, s0, rw, w["tri_chunk"])
    x2 = ffn(x2, 0, 1, (y, g, rw["w_o"]))

    x2 = ffn(x2, 1, 0)
    sb = w["sb"]
    short = t <= 2 * LANES
    q, kt, vt = _sb_pre(x2.reshape(b, t, d), w["mix_g"][1], sb["w_q"], sb["w_kv_t_all" if short else "w_kv_t"],
                        sb["q_g"], sb["k_g_col"], w["eseg"], w["eexp"], streams_inner=short)
    if kt_past is None:
        o = _sb_prompt(q, kt, vt)
    else:
        o = _sb_sample(q, kt, vt, kt_past, vt_past)
    if skip:
        y = _ffn(x2.reshape(b, t, d), w["ffn_g"][1][1], w["ffn_wg"], w["ffn_wu"], w["ffn_wo"], (1, 1),
                 tm_last, (o, None, sb["w_o"]), skip)
    else:
        y = ffn(x2, 1, 1, (o, None, sb["w_o"])).reshape(b, t, d)
    return y, s_fin, shift.reshape(b, d), kt, vt


def kernel(x_prompt, x_sample, state_rwkv_wkv, state_rwkv_shift, cache_sb_k, cache_sb_v, meta_tokens,
           ffn_norm_g, ffn_w_in, ffn_w_out, mix_norm_g, rwkv_mu, rwkv_w_rkv, rwkv_w0, rwkv_w1, rwkv_w2,
           rwkv_a0, rwkv_a1, rwkv_a2, rwkv_g1, rwkv_g2, rwkv_k_k, rwkv_k_a, rwkv_r_k, rwkv_gn_w, rwkv_gn_b,
           rwkv_w_o, sb_w_qkv, sb_q_norm_g, sb_k_norm_g, sb_w_o):
    d = D_MODEL
    depth = ffn_w_in.shape[0]
    d_ff = ffn_w_out.shape[2]
    w_gate = ffn_w_in[..., :d_ff].astype(bf16)
    w_up = ffn_w_in[..., d_ff:].astype(bf16)
    w_out = ffn_w_out.astype(bf16)
    vec = lambda a: a.reshape(1, d).astype(f32)
    head_of_lane = jnp.arange(d) // HEAD_DIM
    eseg = (head_of_lane[:, None] == jnp.arange(LANES)[None, :]).astype(bf16)
    w = {
        "ffn_g": [[vec(ffn_norm_g[i, j]) for j in range(2)] for i in range(depth)],
        "ffn_wg": w_gate, "ffn_wu": w_up, "ffn_wo": w_out,
        "mix_g": [vec(mix_norm_g[i]) for i in range(depth)],
        "eseg": eseg, "eexp": eseg.T,
        "tri_chunk": (jnp.arange(CHUNK)[:, None] >= jnp.arange(CHUNK)[None, :]).astype(bf16),
        "rwkv": {
            "mu": rwkv_mu, "w_rkv": rwkv_w_rkv.astype(bf16), "w0": vec(rwkv_w0), "w1": rwkv_w1.astype(bf16),
            "w2": rwkv_w2.astype(bf16), "a0": vec(rwkv_a0), "a1": rwkv_a1.astype(bf16), "a2": rwkv_a2.astype(bf16),
            "g1": rwkv_g1.astype(bf16), "g2": rwkv_g2.astype(bf16), "k_k": vec(rwkv_k_k), "k_a": vec(rwkv_k_a),
            "r_k": vec(rwkv_r_k), "gn_w": vec(rwkv_gn_w), "gn_b": vec(rwkv_gn_b), "w_o": rwkv_w_o.astype(bf16),
        },
        "sb": {
            "w_q": sb_w_qkv[:, :d].astype(bf16), "w_kv_t": _grouped_kv_t(sb_w_qkv[:, d:], SB_PRE_HEADS).astype(bf16),
            "w_kv_t_all": _grouped_kv_t(sb_w_qkv[:, d:], N_HEADS).astype(bf16),
            "q_g": vec(jnp.tile(sb_q_norm_g, N_HEADS)), "k_g_col": sb_k_norm_g.reshape(HEAD_DIM, 1).astype(f32),
            "w_o": sb_w_o.astype(bf16),
        },
    }

    bp = x_prompt.shape[0]
    s0_p = jnp.zeros((bp, N_HEADS, HEAD_DIM, HEAD_DIM), f32)
    shift0_p = jnp.zeros((bp, d), f32)
    y_prompt, s_p, shift_p, kt_p, vt_p = _stack(x_prompt, s0_p, shift0_p, None, None, w, tm=688, tt=344,
                                                head=meta_tokens.astype(f32), tm_last=512)
    frames_last = lambda a: jnp.swapaxes(a, 2, 3)
    ys, s_s, shift_s, kt_s, vt_s = _stack(x_sample, state_rwkv_wkv, state_rwkv_shift, frames_last(cache_sb_k),
                                          frames_last(cache_sb_v), w, tm=512, tt=64, bs=4)
    return (y_prompt, ys, s_p, shift_p, frames_last(kt_p), frames_last(vt_p), s_s, shift_s,
            frames_last(kt_s), frames_last(vt_s))
```

```python
import functools

import jax
import jax.numpy as jnp
from jax import lax
from jax.experimental import pallas as pl
from jax.experimental.pallas import tpu as pltpu

D_MODEL = 1024
HEAD_DIM = 64
N_HEADS = D_MODEL // HEAD_DIM
RMS_EPS = 1e-6
GN_EPS = 64e-5

LANES = 128
CHUNK = 64
SCAN_CHUNKS_PER_STEP = 3
FF_CHUNK = 256
DEAD_LOG = -104.0
VMEM_LIMIT = 52 * 1024 * 1024

f32 = jnp.float32
bf16 = jnp.bfloat16

_NN = (((1,), (0,)), ((), ()))
_NT = (((1,), (1,)), ((), ()))
_TN = (((0,), (0,)), ((), ()))


def _mm(a, b, dims=_NN):
    return lax.dot_general(a.astype(bf16), b.astype(bf16), dims, preferred_element_type=f32)


def _split(a):
    hi = a.astype(bf16)
    lo = (a - hi.astype(f32)).astype(bf16)
    return hi, lo


def _mm3(a, b, dims=_NN):
    ah, al = _split(a)
    bh, bl = _split(b)
    dot = functools.partial(lax.dot_general, dimension_numbers=dims, preferred_element_type=f32)
    return dot(ah, bh) + (dot(ah, bl) + dot(al, bh))


def _mm_exact_rhs(a, b_bf16, dims=_NN):
    ah, al = _split(a)
    dot = functools.partial(lax.dot_general, dimension_numbers=dims, preferred_element_type=f32)
    return dot(ah, b_bf16) + dot(al, b_bf16)


def _mm_exact_lhs(a_bf16, b, dims=_NN):
    bh, bl = _split(b)
    dot = functools.partial(lax.dot_general, dimension_numbers=dims, preferred_element_type=f32)
    return dot(a_bf16, bh) + dot(a_bf16, bl)


def _rms(x, g):
    return x * lax.rsqrt(jnp.mean(x * x, axis=-1, keepdims=True) + RMS_EPS) * g


def _softplus(y):
    return jnp.maximum(y, 0.0) + jnp.log1p(jnp.exp(-jnp.abs(y)))


def _sigmoid(y):
    return 1.0 / (1.0 + jnp.exp(-y))


def _head_sum(x, eseg_ref, eexp_ref):
    return _mm_exact_rhs(x, eseg_ref[...])


def _head_expand(y, eexp_ref):
    return _mm_exact_rhs(y, eexp_ref[...])


def _full(shape):
    return pl.BlockSpec(shape, lambda *_: (0,) * len(shape))


def _resident(shape):
    return pl.BlockSpec(shape, lambda *_: (0,) * len(shape), pipeline_mode=pl.Buffered(1))


def _params(sem):
    return pltpu.CompilerParams(dimension_semantics=sem, vmem_limit_bytes=VMEM_LIMIT)


def _ffn_kernel(n_mix, n_head, *refs):
    mix, refs = refs[:n_mix], refs[n_mix:]
    head, (x_ref, g_ref, wg_ref, wu_ref, wo_ref, o_ref, acc_ref) = refs[:bool(n_head)], refs[bool(n_head):]
    rows = lambda ref: ref[...].reshape(ref.shape[-2:])
    x = rows(x_ref)
    if n_head:
        moved = jnp.concatenate([head[0][...], x[:x.shape[0] - n_head]], axis=0)
        x = jnp.where(pl.program_id(1) == 0, moved, x)
    if n_mix:
        y = rows(mix[0])
        if n_mix == 3:
            y = y * rows(mix[1])
        x = x + jnp.dot(y.astype(bf16), mix[-1][...], preferred_element_type=f32)
    h = _rms(x, g_ref[...]).astype(bf16)
    d_ff = wg_ref.shape[1]
    for c in range(pl.cdiv(d_ff, FF_CHUNK)):
        sl = slice(c * FF_CHUNK, min((c + 1) * FF_CHUNK, d_ff))
        gate = jnp.dot(h, wg_ref[:, sl], preferred_element_type=f32)
        up = jnp.dot(h, wu_ref[:, sl], preferred_element_type=f32)
        act = (gate * _sigmoid(gate) * up).astype(bf16)
        part = jnp.dot(act, wo_ref[sl, :], preferred_element_type=f32)
        if c == 0:
            acc_ref[...] = part
        else:
            acc_ref[...] += part
    o_ref[...] = x + 0.5 * acc_ref[...]


def _ffn(x, g, wg, wu, wo, ij, tm, mix=None, skip=0, head=None):
    b, t, d = x.shape
    fp = wg.shape[-1]
    picked = lambda *shape: pl.BlockSpec((None, None) + shape, lambda *_: (*ij, 0, 0), pipeline_mode=pl.Buffered(1))
    n_head = 0 if head is None else head.shape[0]
    assert not (n_head and (skip or mix is not None))
    t_out = t - skip + n_head
    assert t_out % tm == 0 and skip % 8 == 0 and n_head % 8 == 0 and tm % 8 == 0
    if skip or n_head:
        row_in = pl.BlockSpec(
            (pl.Element(1), pl.Element(tm), pl.Element(d)),
            lambda bi, i: (bi, pl.multiple_of(jnp.maximum(skip - n_head + i * tm, 0), 8), 0))
    else:
        row_in = pl.BlockSpec((None, tm, d), lambda bi, i: (bi, i, 0))
    row_out = pl.BlockSpec((None, tm, d), lambda bi, i: (bi, i, 0))
    mix_ins, mix_specs = [], []
    if mix is not None:
        y, gate, w_proj = mix
        mix_ins = [y] + ([gate] if gate is not None else []) + [w_proj]
        mix_specs = [row_in] * (len(mix_ins) - 1) + [_resident(w_proj.shape)]
    head_ins, head_specs = ([head], [_resident(head.shape)]) if n_head else ([], [])
    return pl.pallas_call(
        functools.partial(_ffn_kernel, len(mix_ins), n_head),
        grid=(b, t_out // tm),
        in_specs=(mix_specs + head_specs
                  + [row_in, _full((1, d)), picked(d, fp), picked(d, fp), picked(fp, d)]),
        out_specs=row_out,
        out_shape=jax.ShapeDtypeStruct((b, t_out, d), f32),
        scratch_shapes=[pltpu.VMEM((tm, d), f32)],
        compiler_params=_params(("parallel", "parallel")),
        name="ffn",
    )(*mix_ins, *head_ins, x, g, wg, wu, wo)


def _rwkv_pre_kernel(x_ref, shift_ref, ng_ref, mu_ref, wrkv_ref, w0_ref, w1_ref, w2_ref,
                     a0_ref, a1_ref, a2_ref, g1_ref, g2_ref, kk_ref, ka_ref, eseg_ref, eexp_ref,
                     r_out, lw_out, k_out, v_out, kk_out, b_out, g_out, shift_out, prev_ref):
    i = pl.program_id(1)
    bs, tt, d = x_ref.shape

    @pl.when(i == 0)
    def _():
        prev_ref[...] = shift_ref[...]

    h = _rms(x_ref[...].reshape(bs * tt, d), ng_ref[...])
    first_row = lax.broadcasted_iota(jnp.int32, (bs, tt, d), 1).reshape(bs * tt, d) == 0
    carried = jnp.broadcast_to(prev_ref[...], (bs, tt, d)).reshape(bs * tt, d)
    h_prev = jnp.where(first_row, carried, pltpu.roll(h, 1, axis=0))
    last = h.reshape(bs, tt, d)[:, tt - 1:tt, :]
    prev_ref[...] = last
    shift_out[...] = last

    dx = h_prev - h
    xr, xw, xk, xv, xa, xg = ((h + dx * mu_ref[j:j + 1, :]).astype(bf16) for j in range(6))
    r = jnp.dot(xr, wrkv_ref[0], preferred_element_type=f32)
    k = jnp.dot(xk, wrkv_ref[1], preferred_element_type=f32)
    v = jnp.dot(xv, wrkv_ref[2], preferred_element_type=f32)
    lora_w = _mm(jnp.tanh(jnp.dot(xw, w1_ref[...], preferred_element_type=f32)), w2_ref[...])
    log_rate = -_softplus(-(w0_ref[...] + lora_w)) - 0.5
    a = _sigmoid(a0_ref[...] + _mm(jnp.dot(xa, a1_ref[...], preferred_element_type=f32), a2_ref[...]))
    g = _mm(_sigmoid(jnp.dot(xg, g1_ref[...], preferred_element_type=f32)), g2_ref[...])

    kk = k * kk_ref[...]
    norm = jnp.sqrt(_head_sum(kk * kk, eseg_ref, eexp_ref))
    kk = kk * _head_expand(1.0 / jnp.maximum(norm, 1e-12), eexp_ref)

    outs = (r, -jnp.exp(log_rate), k * (1.0 + (a - 1.0) * ka_ref[...]), v, kk, kk * a, g)
    for ref, val in zip((r_out, lw_out, k_out, v_out, kk_out, b_out, g_out), outs):
        ref[...] = val.reshape(bs, tt, d)


def _rwkv_pre(x, shift0, ng, p, eseg, eexp, tt, bs):
    b, t, d = x.shape
    seq = pl.BlockSpec((bs, tt, d), lambda bi, i: (bi, i, 0))
    vec = pl.BlockSpec((bs, 1, d), lambda bi, i: (bi, 0, 0))
    out = jax.ShapeDtypeStruct((b, t, d), f32)
    ins = [x, shift0, ng, p["mu"], p["w_rkv"], p["w0"], p["w1"], p["w2"], p["a0"], p["a1"], p["a2"],
           p["g1"], p["g2"], p["k_k"], p["k_a"], eseg, eexp]
    specs = [seq, vec] + [_resident(a.shape) for a in ins[2:]]
    return pl.pallas_call(
        _rwkv_pre_kernel,
        grid=(b // bs, t // tt),
        in_specs=specs,
        out_specs=[seq] * 7 + [vec],
        out_shape=[out] * 7 + [jax.ShapeDtypeStruct((b, 1, d), f32)],
        scratch_shapes=[pltpu.VMEM((bs, 1, d), f32)],
        compiler_params=_params(("parallel", "arbitrary")),
        name="rwkv_pre",
    )(*ins)


def _rwkv_scan_kernel(t_total, n_sub, r_ref, lw_ref, k_ref, v_ref, kk_ref, b_ref, s0_ref, rk_ref, gnw_ref, gnb_ref,
                      tri_ref, y_ref, s_ref, sbd_ref):
    step = pl.program_id(1)
    C, N = CHUNK, HEAD_DIM
    per_tile = LANES // N
    pairs = range(N_HEADS // per_tile)
    tiles = [slice(p * LANES, (p + 1) * LANES) for p in pairs]

    @pl.when(step == 0)
    def _():
        zero = jnp.zeros((N, N), f32)
        for p in pairs:
            sbd_ref[p] = jnp.concatenate(
                [jnp.concatenate([s0_ref[0, per_tile * p], zero], axis=1),
                 jnp.concatenate([zero, s0_ref[0, per_tile * p + 1]], axis=1)], axis=0)

    row = lax.broadcasted_iota(jnp.int32, (C, 2 * C), 0)
    lane = lax.broadcasted_iota(jnp.int32, (C, 2 * C), 1)
    left = lane < C
    col = jnp.where(left, lane, lane - C)
    strict = row > col
    incl = row >= col
    eye_right = jnp.logical_and(row == col, jnp.logical_not(left)).astype(f32)
    head_lanes = [(lax.broadcasted_iota(jnp.int32, (2 * C, LANES), 1) // N) == s for s in range(per_tile)]
    first = lax.broadcasted_iota(jnp.int32, (C, LANES), 1) < N

    chunks = []
    for j in range(n_sub):
        rows = slice(j * C, (j + 1) * C)
        r, lw, k, v, kk, b = (ref[0, rows] for ref in (r_ref, lw_ref, k_ref, v_ref, kk_ref, b_ref))
        if t_total % (n_sub * C):
            valid = (step * n_sub + j) * C + lax.broadcasted_iota(jnp.int32, r.shape, 0) < t_total
            r, lw, k, v, kk, b = (jnp.where(valid, a, 0.0) for a in (r, lw, k, v, kk, b))
        cum = _mm_exact_lhs(tri_ref[...], lw)
        p_t = jnp.exp(cum)
        p_inv = jnp.exp(-cum)
        r_t = r * p_t
        a_t = -kk * jnp.exp(cum - lw)
        b_t = b * p_inv
        k_t = k * p_inv
        gamma = p_t[C - 1:C, :]
        b_g = b_t * gamma
        k_g = k_t * gamma
        chunks.append(dict(
            rows=rows, gamma=gamma, bonus=r * k * rk_ref[...],
            ar=[jnp.concatenate([a_t[:, sl], r_t[:, sl]], axis=0) for sl in tiles],
            bk=[jnp.concatenate([b_t[:, sl], k_t[:, sl]], axis=0) for sl in tiles],
            bkg=[jnp.concatenate([b_g[:, sl], k_g[:, sl]], axis=0) for sl in tiles],
            v=[v[:, sl] for sl in tiles]))

    chains = [(j, p, s) for j in range(n_sub) for p in pairs for s in range(per_tile)]
    ar = [jnp.where(head_lanes[s], chunks[j]["ar"][p], 0.0) for j, p, s in chains]
    x2 = [_mm(ar[i], chunks[j]["bk"][p], _NT) for i, (j, p, _) in enumerate(chains)]
    z = [jnp.where(jnp.logical_and(strict, left), x[:C], 0.0) + eye_right for x in x2]
    for _ in range(6):
        sq = [_mm(x[:, :C], x) for x in z]
        z = [jnp.where(left, q, x + q) for x, q in zip(z, sq)]
    t_inv = [jnp.where(left, 0.0, x) for x in z]
    a_ak = [jnp.where(jnp.logical_and(strict, jnp.logical_not(left)), x[:C], 0.0) for x in x2]
    akv = [_mm(a_ak[i], jnp.concatenate([chunks[j]["v"][p]] * 2, axis=0)) for i, (j, p, _) in enumerate(chains)]
    a_r = [jnp.where(incl, x[C:], 0.0) for x in x2]

    def head_sum(x):
        s0 = jnp.sum(jnp.where(first, x, 0.0), axis=-1, keepdims=True)
        s1 = jnp.sum(jnp.where(first, 0.0, x), axis=-1, keepdims=True)
        return jnp.where(first, s0, s1)

    for j in range(n_sub):
        ck = chunks[j]
        mine = [i for i, (jj, _, _) in enumerate(chains) if jj == j]
        s_old = [sbd_ref[p] for p in pairs]
        xs = {i: _mm_exact_lhs(ar[i].astype(bf16), s_old[chains[i][1]], _NT) for i in mine}
        rhs = {i: xs[i][:C] + akv[i] for i in mine}
        u = {i: _mm(t_inv[i], jnp.concatenate([rhs[i], rhs[i]], axis=0)) for i in mine}
        uv = {i: jnp.concatenate([u[i], ck["v"][chains[i][1]]], axis=0) for i in mine}
        o = {i: xs[i][C:] + _mm(a_r[i], uv[i]) for i in mine}
        s_add = {i: _mm(uv[i], jnp.where(head_lanes[chains[i][2]], ck["bkg"][chains[i][1]], 0.0), _TN) for i in mine}
        y_tiles = []
        for p in pairs:
            i0, sl = mine[per_tile * p], tiles[p]
            sbd_ref[p] = (s_old[p] * ck["gamma"][:, sl]
                          + jnp.concatenate([s_add[i0][:N], s_add[i0 + 1][N:]], axis=0))
            o_pair = jnp.where(first, o[i0], o[i0 + 1])
            dev = o_pair - head_sum(o_pair) * (1.0 / N)
            o_n = dev * lax.rsqrt(head_sum(dev * dev) * (1.0 / N) + GN_EPS) * gnw_ref[:, sl] + gnb_ref[:, sl]
            y_tiles.append(o_n + head_sum(ck["bonus"][:, sl]) * ck["v"][p])
        y_ref[0, ck["rows"]] = jnp.concatenate(y_tiles, axis=1)

    @pl.when(step == pl.num_programs(1) - 1)
    def _():
        for p in pairs:
            s_ref[0, per_tile * p] = sbd_ref[p][:N, :N]
            s_ref[0, per_tile * p + 1] = sbd_ref[p][N:, N:]


def _rwkv_scan(r, lw, k, v, kk, b, s0, p, tri):
    bsz, t, d = r.shape
    n_sub = min(SCAN_CHUNKS_PER_STEP, pl.cdiv(t, CHUNK))
    seq = pl.BlockSpec((1, n_sub * CHUNK, d), lambda bi, c: (bi, c, 0))
    st = pl.BlockSpec((1, N_HEADS, HEAD_DIM, HEAD_DIM), lambda bi, c: (bi, 0, 0, 0))
    consts = [p["r_k"], p["gn_w"], p["gn_b"], tri]
    return pl.pallas_call(
        functools.partial(_rwkv_scan_kernel, t, n_sub),
        grid=(bsz, pl.cdiv(t, n_sub * CHUNK)),
        in_specs=[seq] * 6 + [st] + [_resident(a.shape) for a in consts],
        out_specs=[seq, st],
        out_shape=[jax.ShapeDtypeStruct((bsz, t, d), f32), jax.ShapeDtypeStruct(s0.shape, f32)],
        scratch_shapes=[pltpu.VMEM((N_HEADS * HEAD_DIM // LANES, LANES, LANES), f32)],
        compiler_params=_params(("parallel", "arbitrary")),
        name="rwkv_scan",
    )(r, lw, k, v, kk, b, s0, *consts)


SB_PRE_HEADS = 4


def _sb_pre_kernel(reuse_h, x_ref, ng_ref, wq_ref, wkv_ref, qg_ref, kg_ref, eseg_ref, eexp_ref, q_out, kt_out, vt_out,
                   h_ref):
    hg, n = kt_out.shape[1], HEAD_DIM
    t = x_ref.shape[1]

    def normed():
        return _rms(x_ref[0], ng_ref[...]).astype(bf16)

    if reuse_h:
        @pl.when(pl.program_id(1) == 0)
        def _():
            h_ref[...] = normed()
        h = h_ref[...]
    else:
        h = normed()
    q = jnp.dot(h, wq_ref[...], preferred_element_type=f32)
    ms = _head_sum(q * q, eseg_ref, eexp_ref) * (1.0 / n)
    q_out[0] = q * _head_expand(lax.rsqrt(ms + RMS_EPS), eexp_ref) * qg_ref[...]

    kv = lax.dot_general(wkv_ref[0], h, _NT, preferred_element_type=f32)
    k = kv[:hg * n].reshape(hg, n, t)
    ms = jnp.mean(k * k, axis=1, keepdims=True)
    kt_out[0] = k * lax.rsqrt(ms + RMS_EPS) * kg_ref[...]
    vt_out[0] = kv[hg * n:].reshape(hg, n, t)


def _sb_pre(x, ng, wq, wkv_t, qg, kg_col, eseg, eexp, streams_inner):
    b, t, d = x.shape
    n = HEAD_DIM
    w = wkv_t.shape[1] // 2
    hg = w // n
    ins = [x, ng, wq, wkv_t, qg, kg_col, eseg[:w], eexp[:, :w]]
    if streams_inner:
        grid = (N_HEADS // hg, b)
        ix = lambda f: (lambda g, bi: f(bi, g))
    else:
        grid = (b, N_HEADS // hg)
        ix = lambda f: f
    tspec = pl.BlockSpec((1, hg, n, t), ix(lambda bi, g: (bi, g, 0, 0)))
    tout = jax.ShapeDtypeStruct((b, N_HEADS, n, t), f32)
    return pl.pallas_call(
        functools.partial(_sb_pre_kernel, not streams_inner),
        grid=grid,
        in_specs=[
            pl.BlockSpec((1, t, d), ix(lambda bi, g: (bi, 0, 0))),
            _resident(ng.shape),
            pl.BlockSpec((d, w), ix(lambda bi, g: (0, g))),
            pl.BlockSpec((1, 2 * w, d), ix(lambda bi, g: (g, 0, 0))),
            pl.BlockSpec((1, w), ix(lambda bi, g: (0, g))),
            _resident(kg_col.shape),
            _resident((w, LANES)),
            _resident((LANES, w)),
        ],
        out_specs=[pl.BlockSpec((1, t, w), ix(lambda bi, g: (bi, 0, g))), tspec, tspec],
        out_shape=[jax.ShapeDtypeStruct((b, t, d), f32), tout, tout],
        scratch_shapes=[pltpu.VMEM((t, d), bf16)],
        compiler_params=_params(("arbitrary", "arbitrary")),
        name="sb_pre",
    )(*ins)


SB_QBLOCK = 128
SB_WINDOW = 3 * SB_QBLOCK
SB_QBLOCKS_PER_STEP = 16
SB_PAST_WINDOW = 256
SB_SAMPLE_HEADS = 16


def _later_matrix(n):
    i = jnp.arange(n)
    return (i[:, None] > i[None, :]).astype(bf16)


def _sb_windows(qs, kts, vts, tri, carries, masks):
    n = range(len(qs))
    z = [_mm(qs[i], kts[i]) * (HEAD_DIM ** -0.5) for i in n]
    log_beta = [jnp.minimum(x, 0.0) - jnp.log(1.0 + jnp.exp(-jnp.abs(x))) for x in z]
    log_1m = [log_beta[i] - z[i] for i in n]
    log_1m = [x if m is None else jnp.where(m, x, 0.0) for x, m in zip(log_1m, masks)]
    later = [_mm(x, tri) for x in log_1m]
    a = [jnp.exp(log_beta[i] + later[i] + carries[i]) for i in n]
    a = [x if m is None else jnp.where(m, x, 0.0) for x, m in zip(a, masks)]
    outs = [_mm(a[i], vts[i], _NT) for i in n]
    return outs, [carries[i] + later[i][:, :1] + log_1m[i][:, :1] for i in n]


def _any_alive(kends, carries):
    flags = [jnp.logical_and(k > 0, jnp.max(c) > DEAD_LOG) for k, c in zip(kends, carries)]
    return functools.reduce(jnp.logical_or, flags)


def _sb_prompt_kernel(q_ref, kt_ref, vt_ref, tri_ref, o_ref):
    step = pl.program_id(2)
    t = q_ref.shape[1]
    N, Q, W, G = HEAD_DIM, SB_QBLOCK, SB_WINDOW, SB_QBLOCKS_PER_STEP
    heads = range(LANES // N)
    tri = tri_ref[...]

    def window(ws, hd):
        return kt_ref[0, hd, :, pl.ds(ws, W)], vt_ref[0, hd, :, pl.ds(ws, W)]

    def earlier_windows(qs, chain_heads, kends, accs, carries):
        col = lax.broadcasted_iota(jnp.int32, (qs[0].shape[0], W), 1)

        def body(st):
            kends, accs, carries = st
            ws = [pl.multiple_of(jnp.maximum(k - W, 0), Q) for k in kends]
            kts, vts = zip(*[window(s, hd) for s, hd in zip(ws, chain_heads)])
            masks = [s + col < k for s, k in zip(ws, kends)]
            parts, carries = _sb_windows(qs, kts, vts, tri, carries, masks)
            return ws, [a + p for a, p in zip(accs, parts)], carries

        return lax.while_loop(lambda st: _any_alive(st[0], st[2]), body, (kends, accs, carries))[1]

    row = lax.broadcasted_iota(jnp.int32, (Q, W), 0)
    col = lax.broadcasted_iota(jnp.int32, (Q, W), 1)
    q0 = [pl.multiple_of((step * G + j) * Q, Q) for j in range(G)]
    ws = [pl.multiple_of(jnp.maximum(x - (W - Q), 0), Q) for x in q0]
    chains = [(j, hd) for j in range(G) for hd in heads]
    chain_heads = [hd for _, hd in chains]
    qs = [q_ref[0, pl.ds(q0[j], Q), hd * N:(hd + 1) * N] for j, hd in chains]
    kts, vts = zip(*[window(ws[j], hd) for j, hd in chains])
    masks = [ws[j] + col < q0[j] + row for j, _ in chains]
    zero = jnp.zeros((Q, 1), f32)
    accs, carries = _sb_windows(qs, kts, vts, tri, [zero] * len(chains), masks)
    accs = earlier_windows(qs, chain_heads, [ws[j] for j, _ in chains], accs, carries)
    for j in range(G):
        o_ref[0, pl.ds(q0[j], Q), :] = jnp.concatenate([accs[j * len(heads) + hd] for hd in heads], axis=1)

    q_full = (t // Q) * Q
    rem = t - q_full
    if rem:
        @pl.when(step == pl.num_programs(2) - 1)
        def _():
            qs = [q_ref[0, q_full:, hd * N:(hd + 1) * N] for hd in heads]
            kts = [kt_ref[0, hd, :, q_full:] for hd in heads]
            vts = [vt_ref[0, hd, :, q_full:] for hd in heads]
            causal = (lax.broadcasted_iota(jnp.int32, (rem, rem), 1)
                      < lax.broadcasted_iota(jnp.int32, (rem, rem), 0))
            zero = jnp.zeros((rem, 1), f32)
            accs, carries = _sb_windows(qs, kts, vts, tri[:rem, :rem], [zero] * len(heads), [causal] * len(heads))
            accs = earlier_windows(qs, list(heads), [jnp.int32(q_full)] * len(heads), accs, carries)
            o_ref[0, q_full:, :] = jnp.concatenate(accs, axis=1)


def _sb_prompt(q, kt, vt):
    b, t, d = q.shape
    n_full = t // SB_QBLOCK
    assert n_full % SB_QBLOCKS_PER_STEP == 0 and t >= SB_WINDOW
    tri = _later_matrix(SB_WINDOW)
    qspec = pl.BlockSpec((1, t, LANES), lambda bi, hp, s: (bi, 0, hp))
    kvspec = pl.BlockSpec((1, LANES // HEAD_DIM, HEAD_DIM, t), lambda bi, hp, s: (bi, hp, 0, 0))
    return pl.pallas_call(
        _sb_prompt_kernel,
        grid=(b, d // LANES, n_full // SB_QBLOCKS_PER_STEP),
        in_specs=[qspec, kvspec, kvspec, _resident(tri.shape)],
        out_specs=qspec,
        out_shape=jax.ShapeDtypeStruct((b, t, d), f32),
        compiler_params=_params(("parallel", "parallel", "arbitrary")),
        name="sb_prompt",
    )(q, kt, vt, tri)


def _sb_sample_kernel(q_ref, ktn_ref, vtn_ref, ktp_ref, vtp_ref, kt_hbm, vt_hbm, tri_ref, o_ref, kbuf, vbuf, sem):
    bi, g = pl.program_id(0), pl.program_id(1)
    N = HEAD_DIM
    hg, t, wp = ktn_ref.shape[1], q_ref.shape[1], ktp_ref.shape[3]
    heads = range(hg)
    tri = tri_ref[...]

    qs = [q_ref[0, :, hd * N:(hd + 1) * N] for hd in heads]
    kts = [jnp.concatenate([ktp_ref[0, hd], ktn_ref[0, hd]], axis=1) for hd in heads]
    vts = [jnp.concatenate([vtp_ref[0, hd], vtn_ref[0, hd]], axis=1) for hd in heads]
    row = lax.broadcasted_iota(jnp.int32, (t, wp + t), 0)
    col = lax.broadcasted_iota(jnp.int32, (t, wp + t), 1)
    mask = col - wp < row
    accs, carries = _sb_windows(qs, kts, vts, tri, [jnp.zeros((t, 1), f32)] * hg, [mask] * hg)

    def fetch(src, dst, ws, slot):
        return pltpu.make_async_copy(src.at[bi, pl.ds(g * hg, hg), :, pl.ds(ws, wp)], dst, sem.at[slot])

    def body(st):
        kend, accs, carries = st
        ws = pl.multiple_of(kend - wp, wp)
        copies = [fetch(kt_hbm, kbuf, ws, 0), fetch(vt_hbm, vbuf, ws, 1)]
        for c in copies:
            c.start()
        for c in copies:
            c.wait()
        parts, carries = _sb_windows(qs, [kbuf[hd] for hd in heads], [vbuf[hd] for hd in heads],
                                     tri[:wp, :wp], carries, [None] * hg)
        return ws, [a + p for a, p in zip(accs, parts)], carries

    kend = jnp.int32(kt_hbm.shape[3] - wp)
    accs = lax.while_loop(lambda st: _any_alive([st[0]] * hg, st[2]), body, (kend, accs, carries))[1]
    o_ref[0] = jnp.concatenate(accs, axis=1)


def _sb_sample(q, kt, vt, kt_past, vt_past):
    b, t, d = q.shape
    hg, n, wp = SB_SAMPLE_HEADS, HEAD_DIM, SB_PAST_WINDOW
    p = kt_past.shape[3]
    assert p % wp == 0
    tri = _later_matrix(wp + t)
    qspec = pl.BlockSpec((1, t, hg * n), lambda bi, g: (bi, 0, g))
    new = pl.BlockSpec((1, hg, n, t), lambda bi, g: (bi, g, 0, 0))
    last = pl.BlockSpec((1, hg, n, wp), lambda bi, g: (bi, g, 0, p // wp - 1))
    hbm = pl.BlockSpec(memory_space=pl.ANY)
    return pl.pallas_call(
        _sb_sample_kernel,
        grid=(b, N_HEADS // hg),
        in_specs=[qspec, new, new, last, last, hbm, hbm, _resident(tri.shape)],
        out_specs=qspec,
        out_shape=jax.ShapeDtypeStruct((b, t, d), f32),
        scratch_shapes=[pltpu.VMEM((hg, n, wp), f32), pltpu.VMEM((hg, n, wp), f32), pltpu.SemaphoreType.DMA((2,))],
        compiler_params=_params(("parallel", "parallel")),
        name="sb_sample",
    )(q, kt, vt, kt_past, vt_past, kt_past, vt_past, tri)


def _grouped_kv_t(w_kv, heads_per_group):
    d = w_kv.shape[0]
    gw = heads_per_group * HEAD_DIM
    k_t = w_kv[:, :d].T.reshape(d // gw, gw, d)
    v_t = w_kv[:, d:].T.reshape(d // gw, gw, d)
    return jnp.concatenate([k_t, v_t], axis=1)


def _stack(x, s0, shift0, kt_past, vt_past, w, tm, tt, bs=1, head=None, tm_last=None):
    b, t0, d = x.shape
    skip = 0 if head is None else head.shape[0]
    t = t0 + skip
    m = b * t
    flat = lambda a: a.reshape(1, m, d)

    def ffn(x2, i, j, mix=None):
        if mix is not None:
            mix = (flat(mix[0]), None if mix[1] is None else flat(mix[1]), mix[2])
        return _ffn(flat(x2), w["ffn_g"][i][j], w["ffn_wg"], w["ffn_wu"], w["ffn_wo"], (i, j),
                    tm, mix).reshape(m, d)

    if head is None:
        x2 = ffn(x.reshape(m, d), 0, 0)
    else:
        x2 = _ffn(x, w["ffn_g"][0][0], w["ffn_wg"], w["ffn_wu"], w["ffn_wo"], (0, 0), tm,
                  head=head).reshape(m, d)
    rw = w["rwkv"]
    r, lw, k, v, kk, bb, g, shift = _rwkv_pre(x2.reshape(b, t, d), shift0.reshape(b, 1, d), w["mix_g"][0], rw,
                                              w["eseg"], w["eexp"], tt, bs)
    y, s_fin = _rwkv_scan(r, lw, k, v, kk, bb, s0, rw, w["tri_chunk"])
    x2 = ffn(x2, 0, 1, (y, g, rw["w_o"]))

    x2 = ffn(x2, 1, 0)
    sb = w["sb"]
    short = t <= 2 * LANES
    q, kt, vt = _sb_pre(x2.reshape(b, t, d), w["mix_g"][1], sb["w_q"], sb["w_kv_t_all" if short else "w_kv_t"],
                        sb["q_g"], sb["k_g_col"], w["eseg"], w["eexp"], streams_inner=short)
    if kt_past is None:
        o = _sb_prompt(q, kt, vt)
    else:
        o = _sb_sample(q, kt, vt, kt_past, vt_past)
    if skip:
        y = _ffn(x2.reshape(b, t, d), w["ffn_g"][1][1], w["ffn_wg"], w["ffn_wu"], w["ffn_wo"], (1, 1),
                 tm_last, (o, None, sb["w_o"]), skip)
    else:
        y = ffn(x2, 1, 1, (o, None, sb["w_o"])).reshape(b, t, d)
    return y, s_fin, shift.reshape(b, d), kt, vt


def kernel(x_prompt, x_sample, state_rwkv_wkv, state_rwkv_shift, cache_sb_k, cache_sb_v, meta_tokens,
           ffn_norm_g, ffn_w_in, ffn_w_out, mix_norm_g, rwkv_mu, rwkv_w_rkv, rwkv_w0, rwkv_w1, rwkv_w2,
           rwkv_a0, rwkv_a1, rwkv_a2, rwkv_g1, rwkv_g2, rwkv_k_k, rwkv_k_a, rwkv_r_k, rwkv_gn_w, rwkv_gn_b,
           rwkv_w_o, sb_w_qkv, sb_q_norm_g, sb_k_norm_g, sb_w_o):
    d = D_MODEL
    depth = ffn_w_in.shape[0]
    d_ff = ffn_w_out.shape[2]
    w_gate = ffn_w_in[..., :d_ff].astype(bf16)
    w_up = ffn_w_in[..., d_ff:].astype(bf16)
    w_out = ffn_w_out.astype(bf16)
    vec = lambda a: a.reshape(1, d).astype(f32)
    head_of_lane = jnp.arange(d) // HEAD_DIM
    eseg = (head_of_lane[:, None] == jnp.arange(LANES)[None, :]).astype(bf16)
    w = {
        "ffn_g": [[vec(ffn_norm_g[i, j]) for j in range(2)] for i in range(depth)],
        "ffn_wg": w_gate, "ffn_wu": w_up, "ffn_wo": w_out,
        "mix_g": [vec(mix_norm_g[i]) for i in range(depth)],
        "eseg": eseg, "eexp": eseg.T,
        "tri_chunk": (jnp.arange(CHUNK)[:, None] >= jnp.arange(CHUNK)[None, :]).astype(bf16),
        "rwkv": {
            "mu": rwkv_mu, "w_rkv": rwkv_w_rkv.astype(bf16), "w0": vec(rwkv_w0), "w1": rwkv_w1.astype(bf16),
            "w2": rwkv_w2.astype(bf16), "a0": vec(rwkv_a0), "a1": rwkv_a1.astype(bf16), "a2": rwkv_a2.astype(bf16),
            "g1": rwkv_g1.astype(bf16), "g2": rwkv_g2.astype(bf16), "k_k": vec(rwkv_k_k), "k_a": vec(rwkv_k_a),
            "r_k": vec(rwkv_r_k), "gn_w": vec(rwkv_gn_w), "gn_b": vec(rwkv_gn_b), "w_o": rwkv_w_o.astype(bf16),
        },
        "sb": {
            "w_q": sb_w_qkv[:, :d].astype(bf16), "w_kv_t": _grouped_kv_t(sb_w_qkv[:, d:], SB_PRE_HEADS).astype(bf16),
            "w_kv_t_all": _grouped_kv_t(sb_w_qkv[:, d:], N_HEADS).astype(bf16),
            "q_g": vec(jnp.tile(sb_q_norm_g, N_HEADS)), "k_g_col": sb_k_norm_g.reshape(HEAD_DIM, 1).astype(f32),
            "w_o": sb_w_o.astype(bf16),
        },
    }

    bp = x_prompt.shape[0]
    s0_p = jnp.zeros((bp, N_HEADS, HEAD_DIM, HEAD_DIM), f32)
    shift0_p = jnp.zeros((bp, d), f32)
    y_prompt, s_p, shift_p, kt_p, vt_p = _stack(x_prompt, s0_p, shift0_p, None, None, w, tm=688, tt=344,
                                                head=meta_tokens.astype(f32), tm_last=512)
    frames_last = lambda a: jnp.swapaxes(a, 2, 3)
    ys, s_s, shift_s, kt_s, vt_s = _stack(x_sample, state_rwkv_wkv, state_rwkv_shift, frames_last(cache_sb_k),
                                          frames_last(cache_sb_v), w, tm=512, tt=64, bs=4)
    return (y_prompt, ys, s_p, shift_p, frames_last(kt_p), frames_last(vt_p), s_s, shift_s,
            frames_last(kt_s), frames_last(vt_s))
```

```python
import functools

import jax
import jax.numpy as jnp
from jax import lax
from jax.experimental import pallas as pl
from jax.experimental.pallas import tpu as pltpu

D_MODEL = 1024
HEAD_DIM = 64
N_HEADS = D_MODEL // HEAD_DIM
RMS_EPS = 1e-6
GN_EPS = 64e-5

LANES = 128
CHUNK = 64
SCAN_CHUNKS_PER_STEP = 3
FF_CHUNK = 256
DEAD_LOG = -104.0
VMEM_LIMIT = 52 * 1024 * 1024

f32 = jnp.float32
bf16 = jnp.bfloat16

_NN = (((1,), (0,)), ((), ()))
_NT = (((1,), (1,)), ((), ()))
_TN = (((0,), (0,)), ((), ()))


def _mm(a, b, dims=_NN):
    return lax.dot_general(a.astype(bf16), b.astype(bf16), dims, preferred_element_type=f32)


def _split(a):
    hi = a.astype(bf16)
    lo = (a - hi.astype(f32)).astype(bf16)
    return hi, lo


def _mm3(a, b, dims=_NN):
    ah, al = _split(a)
    bh, bl = _split(b)
    dot = functools.partial(lax.dot_general, dimension_numbers=dims, preferred_element_type=f32)
    return dot(ah, bh) + (dot(ah, bl) + dot(al, bh))


def _mm_exact_rhs(a, b_bf16, dims=_NN):
    ah, al = _split(a)
    dot = functools.partial(lax.dot_general, dimension_numbers=dims, preferred_element_type=f32)
    return dot(ah, b_bf16) + dot(al, b_bf16)


def _mm_exact_lhs(a_bf16, b, dims=_NN):
    bh, bl = _split(b)
    dot = functools.partial(lax.dot_general, dimension_numbers=dims, preferred_element_type=f32)
    return dot(a_bf16, bh) + dot(a_bf16, bl)


def _rms(x, g):
    return x * lax.rsqrt(jnp.mean(x * x, axis=-1, keepdims=True) + RMS_EPS) * g


def _softplus(y):
    return jnp.maximum(y, 0.0) + jnp.log1p(jnp.exp(-jnp.abs(y)))


def _sigmoid(y):
    return 1.0 / (1.0 + jnp.exp(-y))


def _head_sum(x, eseg_ref, eexp_ref):
    return _mm_exact_rhs(x, eseg_ref[...])


def _head_expand(y, eexp_ref):
    return _mm_exact_rhs(y, eexp_ref[...])


def _full(shape):
    return pl.BlockSpec(shape, lambda *_: (0,) * len(shape))


def _resident(shape):
    return pl.BlockSpec(shape, lambda *_: (0,) * len(shape), pipeline_mode=pl.Buffered(1))


def _params(sem):
    return pltpu.CompilerParams(dimension_semantics=sem, vmem_limit_bytes=VMEM_LIMIT)


def _ffn_kernel(n_mix, n_head, *refs):
    mix, refs = refs[:n_mix], refs[n_mix:]
    head, (x_ref, g_ref, wg_ref, wu_ref, wo_ref, o_ref, acc_ref) = refs[:bool(n_head)], refs[bool(n_head):]
    rows = lambda ref: ref[...].reshape(ref.shape[-2:])
    x = rows(x_ref)
    if n_head:
        moved = jnp.concatenate([head[0][...], x[:x.shape[0] - n_head]], axis=0)
        x = jnp.where(pl.program_id(1) == 0, moved, x)
    if n_mix:
        y = rows(mix[0])
        if n_mix == 3:
            y = y * rows(mix[1])
        x = x + jnp.dot(y.astype(bf16), mix[-1][...], preferred_element_type=f32)
    h = _rms(x, g_ref[...]).astype(bf16)
    d_ff = wg_ref.shape[1]
    for c in range(pl.cdiv(d_ff, FF_CHUNK)):
        sl = slice(c * FF_CHUNK, min((c + 1) * FF_CHUNK, d_ff))
        gate = jnp.dot(h, wg_ref[:, sl], preferred_element_type=f32)
        up = jnp.dot(h, wu_ref[:, sl], preferred_element_type=f32)
        act = (gate * _sigmoid(gate) * up).astype(bf16)
        part = jnp.dot(act, wo_ref[sl, :], preferred_element_type=f32)
        if c == 0:
            acc_ref[...] = part
        else:
            acc_ref[...] += part
    o_ref[...] = x + 0.5 * acc_ref[...]


def _ffn(x, g, wg, wu, wo, ij, tm, mix=None, skip=0, head=None):
    b, t, d = x.shape
    fp = wg.shape[-1]
    picked = lambda *shape: pl.BlockSpec((None, None) + shape, lambda *_: (*ij, 0, 0), pipeline_mode=pl.Buffered(1))
    n_head = 0 if head is None else head.shape[0]
    assert not (n_head and (skip or mix is not None))
    t_out = t - skip + n_head
    assert t_out % tm == 0 and skip % 8 == 0 and n_head % 8 == 0 and tm % 8 == 0
    if skip or n_head:
        row_in = pl.BlockSpec(
            (pl.Element(1), pl.Element(tm), pl.Element(d)),
            lambda bi, i: (bi, pl.multiple_of(jnp.maximum(skip - n_head + i * tm, 0), 8), 0))
    else:
        row_in = pl.BlockSpec((None, tm, d), lambda bi, i: (bi, i, 0))
    row_out = pl.BlockSpec((None, tm, d), lambda bi, i: (bi, i, 0))
    mix_ins, mix_specs = [], []
    if mix is not None:
        y, gate, w_proj = mix
        mix_ins = [y] + ([gate] if gate is not None else []) + [w_proj]
        mix_specs = [row_in] * (len(mix_ins) - 1) + [_resident(w_proj.shape)]
    head_ins, head_specs = ([head], [_resident(head.shape)]) if n_head else ([], [])
    return pl.pallas_call(
        functools.partial(_ffn_kernel, len(mix_ins), n_head),
        grid=(b, t_out // tm),
        in_specs=(mix_specs + head_specs
                  + [row_in, _full((1, d)), picked(d, fp), picked(d, fp), picked(fp, d)]),
        out_specs=row_out,
        out_shape=jax.ShapeDtypeStruct((b, t_out, d), f32),
        scratch_shapes=[pltpu.VMEM((tm, d), f32)],
        compiler_params=_params(("parallel", "parallel")),
        name="ffn",
    )(*mix_ins, *head_ins, x, g, wg, wu, wo)


def _rwkv_pre_kernel(x_ref, shift_ref, ng_ref, mu_ref, wrkv_ref, w0_ref, w1_ref, w2_ref,
                     a0_ref, a1_ref, a2_ref, g1_ref, g2_ref, kk_ref, ka_ref, eseg_ref, eexp_ref,
                     r_out, lw_out, k_out, v_out, kk_out, b_out, g_out, shift_out, prev_ref):
    i = pl.program_id(1)
    bs, tt, d = x_ref.shape

    @pl.when(i == 0)
    def _():
        prev_ref[...] = shift_ref[...]

    h = _rms(x_ref[...].reshape(bs * tt, d), ng_ref[...])
    first_row = lax.broadcasted_iota(jnp.int32, (bs, tt, d), 1).reshape(bs * tt, d) == 0
    carried = jnp.broadcast_to(prev_ref[...], (bs, tt, d)).reshape(bs * tt, d)
    h_prev = jnp.where(first_row, carried, pltpu.roll(h, 1, axis=0))
    last = h.reshape(bs, tt, d)[:, tt - 1:tt, :]
    prev_ref[...] = last
    shift_out[...] = last

    dx = h_prev - h
    xr, xw, xk, xv, xa, xg = ((h + dx * mu_ref[j:j + 1, :]).astype(bf16) for j in range(6))
    r = jnp.dot(xr, wrkv_ref[0], preferred_element_type=f32)
    k = jnp.dot(xk, wrkv_ref[1], preferred_element_type=f32)
    v = jnp.dot(xv, wrkv_ref[2], preferred_element_type=f32)
    lora_w = _mm(jnp.tanh(jnp.dot(xw, w1_ref[...], preferred_element_type=f32)), w2_ref[...])
    log_rate = -_softplus(-(w0_ref[...] + lora_w)) - 0.5
    a = _sigmoid(a0_ref[...] + _mm(jnp.dot(xa, a1_ref[...], preferred_element_type=f32), a2_ref[...]))
    g = _mm(_sigmoid(jnp.dot(xg, g1_ref[...], preferred_element_type=f32)), g2_ref[...])

    kk = k * kk_ref[...]
    norm = jnp.sqrt(_head_sum(kk * kk, eseg_ref, eexp_ref))
    kk = kk * _head_expand(1.0 / jnp.maximum(norm, 1e-12), eexp_ref)

    outs = (r, -jnp.exp(log_rate), k * (1.0 + (a - 1.0) * ka_ref[...]), v, kk, kk * a, g)
    for ref, val in zip((r_out, lw_out, k_out, v_out, kk_out, b_out, g_out), outs):
        ref[...] = val.reshape(bs, tt, d)


def _rwkv_pre(x, shift0, ng, p, eseg, eexp, tt, bs):
    b, t, d = x.shape
    seq = pl.BlockSpec((bs, tt, d), lambda bi, i: (bi, i, 0))
    vec = pl.BlockSpec((bs, 1, d), lambda bi, i: (bi, 0, 0))
    out = jax.ShapeDtypeStruct((b, t, d), f32)
    ins = [x, shift0, ng, p["mu"], p["w_rkv"], p["w0"], p["w1"], p["w2"], p["a0"], p["a1"], p["a2"],
           p["g1"], p["g2"], p["k_k"], p["k_a"], eseg, eexp]
    specs = [seq, vec] + [_resident(a.shape) for a in ins[2:]]
    return pl.pallas_call(
        _rwkv_pre_kernel,
        grid=(b // bs, t // tt),
        in_specs=specs,
        out_specs=[seq] * 7 + [vec],
        out_shape=[out] * 7 + [jax.ShapeDtypeStruct((b, 1, d), f32)],
        scratch_shapes=[pltpu.VMEM((bs, 1, d), f32)],
        compiler_params=_params(("parallel", "arbitrary")),
        name="rwkv_pre",
    )(*ins)


def _rwkv_scan_kernel(t_total, n_sub, r_ref, lw_ref, k_ref, v_ref, kk_ref, b_ref, s0_ref, rk_ref, gnw_ref, gnb_ref,
                      tri_ref, y_ref, s_ref, sbd_ref):
    step = pl.program_id(1)
    C, N = CHUNK, HEAD_DIM
    per_tile = LANES // N
    pairs = range(N_HEADS // per_tile)
    tiles = [slice(p * LANES, (p + 1) * LANES) for p in pairs]

    @pl.when(step == 0)
    def _():
        zero = jnp.zeros((N, N), f32)
        for p in pairs:
            sbd_ref[p] = jnp.concatenate(
                [jnp.concatenate([s0_ref[0, per_tile * p], zero], axis=1),
                 jnp.concatenate([zero, s0_ref[0, per_tile * p + 1]], axis=1)], axis=0)

    row = lax.broadcasted_iota(jnp.int32, (C, 2 * C), 0)
    lane = lax.broadcasted_iota(jnp.int32, (C, 2 * C), 1)
    left = lane < C
    col = jnp.where(left, lane, lane - C)
    strict = row > col
    incl = row >= col
    eye_right = jnp.logical_and(row == col, jnp.logical_not(left)).astype(f32)
    head_lanes = [(lax.broadcasted_iota(jnp.int32, (2 * C, LANES), 1) // N) == s for s in range(per_tile)]
    first = lax.broadcasted_iota(jnp.int32, (C, LANES), 1) < N

    chunks = []
    for j in range(n_sub):
        rows = slice(j * C, (j + 1) * C)
        r, lw, k, v, kk, b = (ref[0, rows] for ref in (r_ref, lw_ref, k_ref, v_ref, kk_ref, b_ref))
        if t_total % (n_sub * C):
            valid = (step * n_sub + j) * C + lax.broadcasted_iota(jnp.int32, r.shape, 0) < t_total
            r, lw, k, v, kk, b = (jnp.where(valid, a, 0.0) for a in (r, lw, k, v, kk, b))
        cum = _mm_exact_lhs(tri_ref[...], lw)
        p_t = jnp.exp(cum)
        p_inv = jnp.exp(-cum)
        r_t = r * p_t
        a_t = -kk * jnp.exp(cum - lw)
        b_t = b * p_inv
        k_t = k * p_inv
        gamma = p_t[C - 1:C, :]
        b_g = b_t * gamma
        k_g = k_t * gamma
        chunks.append(dict(
            rows=rows, gamma=gamma, bonus=r * k * rk_ref[...],
            ar=[jnp.concatenate([a_t[:, sl], r_t[:, sl]], axis=0) for sl in tiles],
            bk=[jnp.concatenate([b_t[:, sl], k_t[:, sl]], axis=0) for sl in tiles],
            bkg=[jnp.concatenate([b_g[:, sl], k_g[:, sl]], axis=0) for sl in tiles],
            v=[v[:, sl] for sl in tiles]))

    chains = [(j, p, s) for j in range(n_sub) for p in pairs for s in range(per_tile)]
    ar = [jnp.where(head_lanes[s], chunks[j]["ar"][p], 0.0) for j, p, s in chains]
    x2 = [_mm(ar[i], chunks[j]["bk"][p], _NT) for i, (j, p, _) in enumerate(chains)]
    z = [jnp.where(jnp.logical_and(strict, left), x[:C], 0.0) + eye_right for x in x2]
    for _ in range(6):
        sq = [_mm(x[:, :C], x) for x in z]
        z = [jnp.where(left, q, x + q) for x, q in zip(z, sq)]
    t_inv = [jnp.where(left, 0.0, x) for x in z]
    a_ak = [jnp.where(jnp.logical_and(strict, jnp.logical_not(left)), x[:C], 0.0) for x in x2]
    akv = [_mm(a_ak[i], jnp.concatenate([chunks[j]["v"][p]] * 2, axis=0)) for i, (j, p, _) in enumerate(chains)]
    a_r = [jnp.where(incl, x[C:], 0.0) for x in x2]

    def head_sum(x):
        s0 = jnp.sum(jnp.where(first, x, 0.0), axis=-1, keepdims=True)
        s1 = jnp.sum(jnp.where(first, 0.0, x), axis=-1, keepdims=True)
        return jnp.where(first, s0, s1)

    for j in range(n_sub):
        ck = chunks[j]
        mine = [i for i, (jj, _, _) in enumerate(chains) if jj == j]
        s_old = [sbd_ref[p] for p in pairs]
        xs = {i: _mm_exact_lhs(ar[i].astype(bf16), s_old[chains[i][1]], _NT) for i in mine}
        rhs = {i: xs[i][:C] + akv[i] for i in mine}
        u = {i: _mm(t_inv[i], jnp.concatenate([rhs[i], rhs[i]], axis=0)) for i in mine}
        uv = {i: jnp.concatenate([u[i], ck["v"][chains[i][1]]], axis=0) for i in mine}
        o = {i: xs[i][C:] + _mm(a_r[i], uv[i]) for i in mine}
        s_add = {i: _mm(uv[i], jnp.where(head_lanes[chains[i][2]], ck["bkg"][chains[i][1]], 0.0), _TN) for i in mine}
        y_tiles = []
        for p in pairs:
            i0, sl = mine[per_tile * p], tiles[p]
            sbd_ref[p] = (s_old[p] * ck["gamma"][:, sl]
                          + jnp.concatenate([s_add[i0][:N], s_add[i0 + 1][N:]], axis=0))
            o_pair = jnp.where(first, o[i0], o[i0 + 1])
            dev = o_pair - head_sum(o_pair) * (1.0 / N)
            o_n = dev * lax.rsqrt(head_sum(dev * dev) * (1.0 / N) + GN_EPS) * gnw_ref[:, sl] + gnb_ref[:, sl]
            y_tiles.append(o_n + head_sum(ck["bonus"][:, sl]) * ck["v"][p])
        y_ref[0, ck["rows"]] = jnp.concatenate(y_tiles, axis=1)

    @pl.when(step == pl.num_programs(1) - 1)
    def _():
        for p in pairs:
            s_ref[0, per_tile * p] = sbd_ref[p][:N, :N]
            s_ref[0, per_tile * p + 1] = sbd_ref[p][N:, N:]


def _rwkv_scan(r, lw, k, v, kk, b, s0, p, tri):
    bsz, t, d = r.shape
    n_sub = min(SCAN_CHUNKS_PER_STEP, pl.cdiv(t, CHUNK))
    seq = pl.BlockSpec((1, n_sub * CHUNK, d), lambda bi, c: (bi, c, 0))
    st = pl.BlockSpec((1, N_HEADS, HEAD_DIM, HEAD_DIM), lambda bi, c: (bi, 0, 0, 0))
    consts = [p["r_k"], p["gn_w"], p["gn_b"], tri]
    return pl.pallas_call(
        functools.partial(_rwkv_scan_kernel, t, n_sub),
        grid=(bsz, pl.cdiv(t, n_sub * CHUNK)),
        in_specs=[seq] * 6 + [st] + [_resident(a.shape) for a in consts],
        out_specs=[seq, st],
        out_shape=[jax.ShapeDtypeStruct((bsz, t, d), f32), jax.ShapeDtypeStruct(s0.shape, f32)],
        scratch_shapes=[pltpu.VMEM((N_HEADS * HEAD_DIM // LANES, LANES, LANES), f32)],
        compiler_params=_params(("parallel", "arbitrary")),
        name="rwkv_scan",
    )(r, lw, k, v, kk, b, s0, *consts)


SB_PRE_HEADS = 4


def _sb_pre_kernel(reuse_h, x_ref, ng_ref, wq_ref, wkv_ref, qg_ref, kg_ref, eseg_ref, eexp_ref, q_out, kt_out, vt_out,
                   h_ref):
    hg, n = kt_out.shape[1], HEAD_DIM
    t = x_ref.shape[1]

    def normed():
        return _rms(x_ref[0], ng_ref[...]).astype(bf16)

    if reuse_h:
        @pl.when(pl.program_id(1) == 0)
        def _():
            h_ref[...] = normed()
        h = h_ref[...]
    else:
        h = normed()
    q = jnp.dot(h, wq_ref[...], preferred_element_type=f32)
    ms = _head_sum(q * q, eseg_ref, eexp_ref) * (1.0 / n)
    q_out[0] = q * _head_expand(lax.rsqrt(ms + RMS_EPS), eexp_ref) * qg_ref[...]

    kv = lax.dot_general(wkv_ref[0], h, _NT, preferred_element_type=f32)
    k = kv[:hg * n].reshape(hg, n, t)
    ms = jnp.mean(k * k, axis=1, keepdims=True)
    kt_out[0] = k * lax.rsqrt(ms + RMS_EPS) * kg_ref[...]
    vt_out[0] = kv[hg * n:].reshape(hg, n, t)


def _sb_pre(x, ng, wq, wkv_t, qg, kg_col, eseg, eexp, streams_inner):
    b, t, d = x.shape
    n = HEAD_DIM
    w = wkv_t.shape[1] // 2
    hg = w // n
    ins = [x, ng, wq, wkv_t, qg, kg_col, eseg[:w], eexp[:, :w]]
    if streams_inner:
        grid = (N_HEADS // hg, b)
        ix = lambda f: (lambda g, bi: f(bi, g))
    else:
        grid = (b, N_HEADS // hg)
        ix = lambda f: f
    tspec = pl.BlockSpec((1, hg, n, t), ix(lambda bi, g: (bi, g, 0, 0)))
    tout = jax.ShapeDtypeStruct((b, N_HEADS, n, t), f32)
    return pl.pallas_call(
        functools.partial(_sb_pre_kernel, not streams_inner),
        grid=grid,
        in_specs=[
            pl.BlockSpec((1, t, d), ix(lambda bi, g: (bi, 0, 0))),
            _resident(ng.shape),
            pl.BlockSpec((d, w), ix(lambda bi, g: (0, g))),
            pl.BlockSpec((1, 2 * w, d), ix(lambda bi, g: (g, 0, 0))),
            pl.BlockSpec((1, w), ix(lambda bi, g: (0, g))),
            _resident(kg_col.shape),
            _resident((w, LANES)),
            _resident((LANES, w)),
        ],
        out_specs=[pl.BlockSpec((1, t, w), ix(lambda bi, g: (bi, 0, g))), tspec, tspec],
        out_shape=[jax.ShapeDtypeStruct((b, t, d), f32), tout, tout],
        scratch_shapes=[pltpu.VMEM((t, d), bf16)],
        compiler_params=_params(("arbitrary", "arbitrary")),
        name="sb_pre",
    )(*ins)


SB_QBLOCK = 128
SB_WINDOW = 3 * SB_QBLOCK
SB_QBLOCKS_PER_STEP = 16
SB_MASKED_Z = -1e4
SB_PAST_WINDOW = 256
SB_SAMPLE_HEADS = 16


def _later_matrix(n):
    i = jnp.arange(n)
    return (i[:, None] > i[None, :]).astype(bf16)


def _sb_windows(qs, kts, vts, tri, carries, masks):
    n = range(len(qs))
    z = [_mm(qs[i], kts[i]) * (HEAD_DIM ** -0.5) for i in n]
    z = [x if m is None else jnp.where(m, x, SB_MASKED_Z) for x, m in zip(z, masks)]
    log_beta = [jnp.minimum(x, 0.0) - jnp.log(1.0 + jnp.exp(-jnp.abs(x))) for x in z]
    log_1m = [log_beta[i] - z[i] for i in n]
    later = [_mm(x, tri) for x in log_1m]
    a = [jnp.exp(log_beta[i] + later[i] + carries[i]) for i in n]
    outs = [_mm(a[i], vts[i], _NT) for i in n]
    return outs, [carries[i] + later[i][:, :1] + log_1m[i][:, :1] for i in n]


def _any_alive(kends, carries):
    flags = [jnp.logical_and(k > 0, jnp.max(c) > DEAD_LOG) for k, c in zip(kends, carries)]
    return functools.reduce(jnp.logical_or, flags)


def _sb_prompt_kernel(q_ref, kt_ref, vt_ref, tri_ref, o_ref):
    step = pl.program_id(2)
    t = q_ref.shape[1]
    N, Q, W, G = HEAD_DIM, SB_QBLOCK, SB_WINDOW, SB_QBLOCKS_PER_STEP
    heads = range(LANES // N)
    tri = tri_ref[...]

    def window(ws, hd):
        return kt_ref[0, hd, :, pl.ds(ws, W)], vt_ref[0, hd, :, pl.ds(ws, W)]

    def earlier_windows(qs, chain_heads, kends, accs, carries):
        col = lax.broadcasted_iota(jnp.int32, (qs[0].shape[0], W), 1)

        def body(st):
            kends, accs, carries = st
            ws = [pl.multiple_of(jnp.maximum(k - W, 0), Q) for k in kends]
            kts, vts = zip(*[window(s, hd) for s, hd in zip(ws, chain_heads)])
            masks = [s + col < k for s, k in zip(ws, kends)]
            parts, carries = _sb_windows(qs, kts, vts, tri, carries, masks)
            return ws, [a + p for a, p in zip(accs, parts)], carries

        return lax.while_loop(lambda st: _any_alive(st[0], st[2]), body, (kends, accs, carries))[1]

    row = lax.broadcasted_iota(jnp.int32, (Q, W), 0)
    col = lax.broadcasted_iota(jnp.int32, (Q, W), 1)
    q0 = [pl.multiple_of((step * G + j) * Q, Q) for j in range(G)]
    ws = [pl.multiple_of(jnp.maximum(x - (W - Q), 0), Q) for x in q0]
    chains = [(j, hd) for j in range(G) for hd in heads]
    chain_heads = [hd for _, hd in chains]
    qs = [q_ref[0, pl.ds(q0[j], Q), hd * N:(hd + 1) * N] for j, hd in chains]
    kts, vts = zip(*[window(ws[j], hd) for j, hd in chains])
    masks = [ws[j] + col < q0[j] + row for j, _ in chains]
    zero = jnp.zeros((Q, 1), f32)
    accs, carries = _sb_windows(qs, kts, vts, tri, [zero] * len(chains), masks)
    accs = earlier_windows(qs, chain_heads, [ws[j] for j, _ in chains], accs, carries)
    for j in range(G):
        o_ref[0, pl.ds(q0[j], Q), :] = jnp.concatenate([accs[j * len(heads) + hd] for hd in heads], axis=1)

    q_full = (t // Q) * Q
    rem = t - q_full
    if rem:
        @pl.when(step == pl.num_programs(2) - 1)
        def _():
            qs = [q_ref[0, q_full:, hd * N:(hd + 1) * N] for hd in heads]
            kts = [kt_ref[0, hd, :, q_full:] for hd in heads]
            vts = [vt_ref[0, hd, :, q_full:] for hd in heads]
            causal = (lax.broadcasted_iota(jnp.int32, (rem, rem), 1)
                      < lax.broadcasted_iota(jnp.int32, (rem, rem), 0))
            zero = jnp.zeros((rem, 1), f32)
            accs, carries = _sb_windows(qs, kts, vts, tri[:rem, :rem], [zero] * len(heads), [causal] * len(heads))
            accs = earlier_windows(qs, list(heads), [jnp.int32(q_full)] * len(heads), accs, carries)
            o_ref[0, q_full:, :] = jnp.concatenate(accs, axis=1)


def _sb_prompt(q, kt, vt):
    b, t, d = q.shape
    n_full = t // SB_QBLOCK
    assert n_full % SB_QBLOCKS_PER_STEP == 0 and t >= SB_WINDOW
    tri = _later_matrix(SB_WINDOW)
    qspec = pl.BlockSpec((1, t, LANES), lambda bi, hp, s: (bi, 0, hp))
    kvspec = pl.BlockSpec((1, LANES // HEAD_DIM, HEAD_DIM, t), lambda bi, hp, s: (bi, hp, 0, 0))
    return pl.pallas_call(
        _sb_prompt_kernel,
        grid=(b, d // LANES, n_full // SB_QBLOCKS_PER_STEP),
        in_specs=[qspec, kvspec, kvspec, _resident(tri.shape)],
        out_specs=qspec,
        out_shape=jax.ShapeDtypeStruct((b, t, d), f32),
        compiler_params=_params(("parallel", "parallel", "arbitrary")),
        name="sb_prompt",
    )(q, kt, vt, tri)


def _sb_sample_kernel(q_ref, ktn_ref, vtn_ref, ktp_ref, vtp_ref, kt_hbm, vt_hbm, tri_ref, o_ref, kbuf, vbuf, sem):
    bi, g = pl.program_id(0), pl.program_id(1)
    N = HEAD_DIM
    hg, t, wp = ktn_ref.shape[1], q_ref.shape[1], ktp_ref.shape[3]
    heads = range(hg)
    tri = tri_ref[...]

    qs = [q_ref[0, :, hd * N:(hd + 1) * N] for hd in heads]
    kts = [jnp.concatenate([ktp_ref[0, hd], ktn_ref[0, hd]], axis=1) for hd in heads]
    vts = [jnp.concatenate([vtp_ref[0, hd], vtn_ref[0, hd]], axis=1) for hd in heads]
    row = lax.broadcasted_iota(jnp.int32, (t, wp + t), 0)
    col = lax.broadcasted_iota(jnp.int32, (t, wp + t), 1)
    mask = col - wp < row
    accs, carries = _sb_windows(qs, kts, vts, tri, [jnp.zeros((t, 1), f32)] * hg, [mask] * hg)

    def fetch(src, dst, ws, slot):
        return pltpu.make_async_copy(src.at[bi, pl.ds(g * hg, hg), :, pl.ds(ws, wp)], dst, sem.at[slot])

    def body(st):
        kend, accs, carries = st
        ws = pl.multiple_of(kend - wp, wp)
        copies = [fetch(kt_hbm, kbuf, ws, 0), fetch(vt_hbm, vbuf, ws, 1)]
        for c in copies:
            c.start()
        for c in copies:
            c.wait()
        parts, carries = _sb_windows(qs, [kbuf[hd] for hd in heads], [vbuf[hd] for hd in heads],
                                     tri[:wp, :wp], carries, [None] * hg)
        return ws, [a + p for a, p in zip(accs, parts)], carries

    kend = jnp.int32(kt_hbm.shape[3] - wp)
    accs = lax.while_loop(lambda st: _any_alive([st[0]] * hg, st[2]), body, (kend, accs, carries))[1]
    o_ref[0] = jnp.concatenate(accs, axis=1)


def _sb_sample(q, kt, vt, kt_past, vt_past):
    b, t, d = q.shape
    hg, n, wp = SB_SAMPLE_HEADS, HEAD_DIM, SB_PAST_WINDOW
    p = kt_past.shape[3]
    assert p % wp == 0
    tri = _later_matrix(wp + t)
    qspec = pl.BlockSpec((1, t, hg * n), lambda bi, g: (bi, 0, g))
    new = pl.BlockSpec((1, hg, n, t), lambda bi, g: (bi, g, 0, 0))
    last = pl.BlockSpec((1, hg, n, wp), lambda bi, g: (bi, g, 0, p // wp - 1))
    hbm = pl.BlockSpec(memory_space=pl.ANY)
    return pl.pallas_call(
        _sb_sample_kernel,
        grid=(b, N_HEADS // hg),
        in_specs=[qspec, new, new, last, last, hbm, hbm, _resident(tri.shape)],
        out_specs=qspec,
        out_shape=jax.ShapeDtypeStruct((b, t, d), f32),
        scratch_shapes=[pltpu.VMEM((hg, n, wp), f32), pltpu.VMEM((hg, n, wp), f32), pltpu.SemaphoreType.DMA((2,))],
        compiler_params=_params(("parallel", "parallel")),
        name="sb_sample",
    )(q, kt, vt, kt_past, vt_past, kt_past, vt_past, tri)


def _grouped_kv_t(w_kv, heads_per_group):
    d = w_kv.shape[0]
    gw = heads_per_group * HEAD_DIM
    k_t = w_kv[:, :d].T.reshape(d // gw, gw, d)
    v_t = w_kv[:, d:].T.reshape(d // gw, gw, d)
    return jnp.concatenate([k_t, v_t], axis=1)


def _stack(x, s0, shift0, kt_past, vt_past, w, tm, tt, bs=1, head=None, tm_last=None):
    b, t0, d = x.shape
    skip = 0 if head is None else head.shape[0]
    t = t0 + skip
    m = b * t
    flat = lambda a: a.reshape(1, m, d)

    def ffn(x2, i, j, mix=None):
        if mix is not None:
            mix = (flat(mix[0]), None if mix[1] is None else flat(mix[1]), mix[2])
        return _ffn(flat(x2), w["ffn_g"][i][j], w["ffn_wg"], w["ffn_wu"], w["ffn_wo"], (i, j),
                    tm, mix).reshape(m, d)

    if head is None:
        x2 = ffn(x.reshape(m, d), 0, 0)
    else:
        x2 = _ffn(x, w["ffn_g"][0][0], w["ffn_wg"], w["ffn_wu"], w["ffn_wo"], (0, 0), tm,
                  head=head).reshape(m, d)
    rw = w["rwkv"]
    r, lw, k, v, kk, bb, g, shift = _rwkv_pre(x2.reshape(b, t, d), shift0.reshape(b, 1, d), w["mix_g"][0], rw,
                                              w["eseg"], w["eexp"], tt, bs)
    y, s_fin = _rwkv_scan(r, lw, k, v, kk, bb, s0, rw, w["tri_chunk"])
    x2 = ffn(x2, 0, 1, (y, g, rw["w_o"]))

    x2 = ffn(x2, 1, 0)
    sb = w["sb"]
    short = t <= 2 * LANES
    q, kt, vt = _sb_pre(x2.reshape(b, t, d), w["mix_g"][1], sb["w_q"], sb["w_kv_t_all" if short else "w_kv_t"],
                        sb["q_g"], sb["k_g_col"], w["eseg"], w["eexp"], streams_inner=short)
    if kt_past is None:
        o = _sb_prompt(q, kt, vt)
    else:
        o = _sb_sample(q, kt, vt, kt_past, vt_past)
    if skip:
        y = _ffn(x2.reshape(b, t, d), w["ffn_g"][1][1], w["ffn_wg"], w["ffn_wu"], w["ffn_wo"], (1, 1),
                 tm_last, (o, None, sb["w_o"]), skip)
    else:
        y = ffn(x2, 1, 1, (o, None, sb["w_o"])).reshape(b, t, d)
    return y, s_fin, shift.reshape(b, d), kt, vt


def kernel(x_prompt, x_sample, state_rwkv_wkv, state_rwkv_shift, cache_sb_k, cache_sb_v, meta_tokens,
           ffn_norm_g, ffn_w_in, ffn_w_out, mix_norm_g, rwkv_mu, rwkv_w_rkv, rwkv_w0, rwkv_w1, rwkv_w2,
           rwkv_a0, rwkv_a1, rwkv_a2, rwkv_g1, rwkv_g2, rwkv_k_k, rwkv_k_a, rwkv_r_k, rwkv_gn_w, rwkv_gn_b,
           rwkv_w_o, sb_w_qkv, sb_q_norm_g, sb_k_norm_g, sb_w_o):
    d = D_MODEL
    depth = ffn_w_in.shape[0]
    d_ff = ffn_w_out.shape[2]
    w_gate = ffn_w_in[..., :d_ff].astype(bf16)
    w_up = ffn_w_in[..., d_ff:].astype(bf16)
    w_out = ffn_w_out.astype(bf16)
    vec = lambda a: a.reshape(1, d).astype(f32)
    head_of_lane = jnp.arange(d) // HEAD_DIM
    eseg = (head_of_lane[:, None] == jnp.arange(LANES)[None, :]).astype(bf16)
    w = {
        "ffn_g": [[vec(ffn_norm_g[i, j]) for j in range(2)] for i in range(depth)],
        "ffn_wg": w_gate, "ffn_wu": w_up, "ffn_wo": w_out,
        "mix_g": [vec(mix_norm_g[i]) for i in range(depth)],
        "eseg": eseg, "eexp": eseg.T,
        "tri_chunk": (jnp.arange(CHUNK)[:, None] >= jnp.arange(CHUNK)[None, :]).astype(bf16),
        "rwkv": {
            "mu": rwkv_mu, "w_rkv": rwkv_w_rkv.astype(bf16), "w0": vec(rwkv_w0), "w1": rwkv_w1.astype(bf16),
            "w2": rwkv_w2.astype(bf16), "a0": vec(rwkv_a0), "a1": rwkv_a1.astype(bf16), "a2": rwkv_a2.astype(bf16),
            "g1": rwkv_g1.astype(bf16), "g2": rwkv_g2.astype(bf16), "k_k": vec(rwkv_k_k), "k_a": vec(rwkv_k_a),
            "r_k": vec(rwkv_r_k), "gn_w": vec(rwkv_gn_w), "gn_b": vec(rwkv_gn_b), "w_o": rwkv_w_o.astype(bf16),
        },
        "sb": {
            "w_q": sb_w_qkv[:, :d].astype(bf16), "w_kv_t": _grouped_kv_t(sb_w_qkv[:, d:], SB_PRE_HEADS).astype(bf16),
            "w_kv_t_all": _grouped_kv_t(sb_w_qkv[:, d:], N_HEADS).astype(bf16),
            "q_g": vec(jnp.tile(sb_q_norm_g, N_HEADS)), "k_g_col": sb_k_norm_g.reshape(HEAD_DIM, 1).astype(f32),
            "w_o": sb_w_o.astype(bf16),
        },
    }

    bp = x_prompt.shape[0]
    s0_p = jnp.zeros((bp, N_HEADS, HEAD_DIM, HEAD_DIM), f32)
    shift0_p = jnp.zeros((bp, d), f32)
    y_prompt, s_p, shift_p, kt_p, vt_p = _stack(x_prompt, s0_p, shift0_p, None, None, w, tm=688, tt=344,
                                                head=meta_tokens.astype(f32), tm_last=512)
    frames_last = lambda a: jnp.swapaxes(a, 2, 3)
    ys, s_s, shift_s, kt_s, vt_s = _stack(x_sample, state_rwkv_wkv, state_rwkv_shift, frames_last(cache_sb_k),
                                          frames_last(cache_sb_v), w, tm=512, tt=64, bs=4)
    return (y_prompt, ys, s_p, shift_p, frames_last(kt_p), frames_last(vt_p), s_s, shift_s,
            frames_last(kt_s), frames_last(vt_s))
```
